```python
import jax
import jax.numpy as jnp
from jax import lax
import numpy as np

D_MODEL = 1024
BATCH = 16
SEQ = 2048
DEPTH = 1

MIX_WIDTH = D_MODEL
ATTN_WIDTH = MIX_WIDTH // 2
ATTN_HEAD_DIM = 64
ATTN_HEADS = ATTN_WIDTH // ATTN_HEAD_DIM
DILATED_PATTERNS = ((128, 1), (512, 4), (2048, 16))
MLSTM_WIDTH = MIX_WIDTH - ATTN_WIDTH
MLSTM_HEADS = 4
MLSTM_HEAD_DIM = MLSTM_WIDTH // MLSTM_HEADS
MLSTM_CONV = 4
MLSTM_CHUNK = 128
PEER_HEADS = 8
PEER_N_KEYS = 128
PEER_N_EXPERTS = PEER_N_KEYS * PEER_N_KEYS
PEER_KEY_DIM = 256
PEER_TOPK = 16
PEER_TOKEN_BLOCK = 128
DEEPNORM_ALPHA = (2.0 * DEPTH) ** 0.25
DEEPNORM_BETA = (8.0 * DEPTH) ** -0.25
LN_EPS = 1e-5
IN_SIZES = (ATTN_WIDTH, ATTN_WIDTH, ATTN_WIDTH, MLSTM_WIDTH, MLSTM_WIDTH, MLSTM_WIDTH, 2 * MLSTM_HEADS)
IN_COLS = sum(IN_SIZES)
SPLIT_POINTS = tuple(int(s) for s in np.cumsum(IN_SIZES)[:-1])

kernel_name = "hymba_dilated_mlstm_peer_deepnorm"


def _layer_norm(x, g, b):
    xf = x.astype(jnp.float32)
    mu = jnp.mean(xf, axis=-1, keepdims=True)
    var = jnp.mean(jnp.square(xf - mu), axis=-1, keepdims=True)
    y = (xf - mu) * lax.rsqrt(var + LN_EPS)
    return (y * g.astype(jnp.float32) + b.astype(jnp.float32)).astype(x.dtype)


def _dilated_branch(q, k, v, window, dilation):
    B, S, H, E = q.shape
    steps = window // dilation
    L = S // dilation
    n_blk = -(-L // steps)
    pad = n_blk * steps - L

    def to_sub(t):
        return t.reshape(B, L, dilation, H, E).transpose(0, 2, 3, 1, 4)

    qb = jnp.pad(to_sub(q), ((0, 0), (0, 0), (0, 0), (0, pad), (0, 0))).reshape(B, dilation, H, n_blk, steps, E)
    kp = jnp.pad(to_sub(k), ((0, 0), (0, 0), (0, 0), (steps, pad), (0, 0))).reshape(B, dilation, H, n_blk + 1, steps, E)
    vp = jnp.pad(to_sub(v), ((0, 0), (0, 0), (0, 0), (steps, pad), (0, 0))).reshape(B, dilation, H, n_blk + 1, steps, E)
    kb = jnp.concatenate([kp[:, :, :, :-1], kp[:, :, :, 1:]], axis=4)
    vb = jnp.concatenate([vp[:, :, :, :-1], vp[:, :, :, 1:]], axis=4)
    s = jnp.einsum('bghnqe,bghnke->bghnqk', qb, kb) * (E ** -0.5)
    qpos = jnp.arange(n_blk)[:, None] * steps + jnp.arange(steps)[None, :]
    kpos = jnp.arange(n_blk)[:, None] * steps - steps + jnp.arange(2 * steps)[None, :]
    dist = qpos[:, :, None] - kpos[:, None, :]
    mask = (dist >= 0) & (dist <= steps) & (kpos[:, None, :] >= 0)
    s = jnp.where(mask, s, -jnp.inf)
    m = jnp.max(s, axis=-1)
    p = jnp.exp(s - m[..., None])
    l = jnp.sum(p, axis=-1)
    acc = jnp.einsum('bghnqk,bghnke->bghnqe', p, vb)

    def from_sub(t):
        rest = t.shape[5:]
        t = t.reshape((B, dilation, H, n_blk * steps) + rest)[:, :, :, :L]
        perm = (0, 3, 1, 2) + tuple(range(4, 4 + len(rest)))
        return t.transpose(perm).reshape((B, S, H) + rest)

    return from_sub(acc), from_sub(m), from_sub(l)


def _dilated_attention(q, k, v):
    branches = [_dilated_branch(q, k, v, w, d) for (w, d) in DILATED_PATTERNS]
    m_all = jnp.max(jnp.stack([b[1] for b in branches]), axis=0)
    num = 0.0
    den = 0.0
    for acc, m, l in branches:
        e = jnp.exp(m - m_all)
        num = num + acc * e[..., None]
        den = den + l * e
    return num / den[..., None]


def _mlstm_chunkwise(q, k, v, i_pre, logf):
    B, S, H, E = q.shape
    NC = S // MLSTM_CHUNK
    CH = MLSTM_CHUNK

    def chunks(t):
        return t.reshape(B, NC, CH, H, E).transpose(1, 0, 3, 2, 4)

    def gchunks(t):
        return t.reshape(B, NC, CH, H).transpose(1, 0, 3, 2)

    bcum = jnp.cumsum(gchunks(logf), axis=-1)
    tril = jnp.tril(jnp.ones((CH, CH), dtype=bool))

    def step(carry, xs):
        C, n, m = carry
        qc, kc, vc, ic, bc = xs
        dlog = jnp.where(tril, bc[..., :, None] - bc[..., None, :] + ic[..., None, :], -jnp.inf)
        m_inter = bc + m[..., None]
        m_t = jnp.maximum(m_inter, jnp.max(dlog, axis=-1))
        w_inter = jnp.exp(m_inter - m_t)
        P = jnp.exp(dlog - m_t[..., None]) * jnp.einsum('bhte,bhse->bhts', qc, kc)
        num = w_inter[..., None] * jnp.einsum('bhvk,bhtk->bhtv', C, qc) + jnp.einsum('bhts,bhsv->bhtv', P, vc)
        den = w_inter * jnp.einsum('bhtk,bhk->bht', qc, n) + jnp.sum(P, axis=-1)
        h = num / jnp.maximum(jnp.abs(den), jnp.exp(-m_t))[..., None]
        bL = bc[..., -1]
        g = bL[..., None] - bc + ic
        m_new = jnp.maximum(bL + m, jnp.max(g, axis=-1))
        a = jnp.exp(bL + m - m_new)
        wk = jnp.exp(g - m_new[..., None])
        C_new = a[..., None, None] * C + jnp.einsum('bhs,bhsv,bhsk->bhvk', wk, vc, kc)
        n_new = a[..., None] * n + jnp.einsum('bhs,bhsk->bhk', wk, kc)
        return (C_new, n_new, m_new), h

    init = (jnp.zeros((B, H, E, E), jnp.float32), jnp.zeros((B, H, E), jnp.float32), jnp.zeros((B, H), jnp.float32))
    _, hs = lax.scan(step, init, (chunks(q), chunks(k), chunks(v), gchunks(i_pre), bcum))
    return hs.transpose(1, 0, 3, 2, 4).reshape(B, S, H, E)


def _mlstm_mixer(x_m, v_m, o_m, gates, conv_w, conv_b, w_mq, w_mk, b_igate, b_fgate, mh_norm_g):
    B, S, _ = x_m.shape
    H, E = MLSTM_HEADS, MLSTM_HEAD_DIM
    xc = lax.conv_general_dilated(x_m, conv_w[:, None, :], window_strides=(1,), padding=((MLSTM_CONV - 1, 0),),
                                  dimension_numbers=('NWC', 'WIO', 'NWC'), feature_group_count=MLSTM_WIDTH) + conv_b
    xc = jax.nn.silu(xc).reshape(B, S, H, E)
    q = jnp.einsum('bshe,hef->bshf', xc, w_mq).astype(jnp.float32) * (E ** -0.5)
    k = jnp.einsum('bshe,hef->bshf', xc, w_mk).astype(jnp.float32)
    v = v_m.reshape(B, S, H, E).astype(jnp.float32)
    g = gates.astype(jnp.float32)
    i_pre = g[..., :H] + b_igate.astype(jnp.float32)
    logf = jax.nn.log_sigmoid(g[..., H:] + b_fgate.astype(jnp.float32))
    ht = _mlstm_chunkwise(q, k, v, i_pre, logf)
    mu = jnp.mean(ht, axis=-1, keepdims=True)
    var = jnp.mean(jnp.square(ht - mu), axis=-1, keepdims=True)
    hn = ((ht - mu) * lax.rsqrt(var + LN_EPS)).reshape(B, S, MLSTM_WIDTH) * mh_norm_g.astype(jnp.float32)
    out = jax.nn.sigmoid(o_m.astype(jnp.float32)) * hn
    return out.astype(x_m.dtype)


def _token_mixing(h, w_in, conv_w, conv_b, w_mq, w_mk, b_igate, b_fgate, mh_norm_g, w_out):
    B, S, _ = h.shape
    proj = h @ w_in
    q_a, k_a, v_a, x_m, v_m, o_m, gates = jnp.split(proj, SPLIT_POINTS, axis=-1)
    shp = (B, S, ATTN_HEADS, ATTN_HEAD_DIM)
    attn = _dilated_attention(q_a.reshape(shp).astype(jnp.float32), k_a.reshape(shp).astype(jnp.float32),
                              v_a.reshape(shp).astype(jnp.float32))
    attn = attn.reshape(B, S, ATTN_WIDTH).astype(h.dtype)
    mlstm = _mlstm_mixer(x_m, v_m, o_m, gates, conv_w, conv_b, w_mq, w_mk, b_igate, b_fgate, mh_norm_g)
    return jnp.concatenate([attn, mlstm], axis=-1) @ w_out


def _peer(h, w_query, sub_keys, w_down, w_up):
    B, S, D = h.shape
    T = B * S
    xt = h.reshape(T, D)
    qry = (xt @ w_query).reshape(T, PEER_HEADS, 2, PEER_KEY_DIM // 2).astype(jnp.float32)
    sc = jnp.einsum('thpc,hpkc->thpk', qry, sub_keys.astype(jnp.float32))
    v1, i1 = lax.top_k(sc[:, :, 0], PEER_TOPK)
    v2, i2 = lax.top_k(sc[:, :, 1], PEER_TOPK)
    cand = (v1[..., :, None] + v2[..., None, :]).reshape(T, PEER_HEADS, PEER_TOPK * PEER_TOPK)
    cidx = (i1[..., :, None] * PEER_N_KEYS + i2[..., None, :]).reshape(T, PEER_HEADS, PEER_TOPK * PEER_TOPK)
    best, pos = lax.top_k(cand, PEER_TOPK)
    eidx = jnp.take_along_axis(cidx, pos, axis=-1).reshape(T, PEER_HEADS * PEER_TOPK)
    gate = jax.nn.softmax(best, axis=-1).reshape(T, PEER_HEADS * PEER_TOPK)
    NB = T // PEER_TOKEN_BLOCK

    def expert_block(args):
        xb, ib, gb = args
        u = w_down[ib]
        a = jnp.einsum('tkd,td->tk', u, xb).astype(jnp.float32)
        a = jax.nn.gelu(a, approximate=False) * gb
        vsel = w_up[ib]
        return jnp.einsum('tk,tkd->td', a.astype(vsel.dtype), vsel)

    out = lax.map(expert_block, (xt.reshape(NB, PEER_TOKEN_BLOCK, D),
                                 eidx.reshape(NB, PEER_TOKEN_BLOCK, -1),
                                 gate.reshape(NB, PEER_TOKEN_BLOCK, -1)))
    return out.reshape(B, S, D).astype(h.dtype)


def setup_inputs(seed: int = 0) -> dict:
    key = jax.random.key(seed)
    ks = jax.random.split(key, 24)
    D = D_MODEL
    H, E = MLSTM_HEADS, MLSTM_HEAD_DIM
    nrm = jax.random.normal
    std = D ** -0.5
    x = nrm(ks[0], (BATCH, SEQ, D), jnp.float32)
    w_qk = nrm(ks[1], (DEPTH, D, 2 * ATTN_WIDTH)) * std
    w_va = nrm(ks[2], (DEPTH, D, ATTN_WIDTH)) * std * DEEPNORM_BETA
    w_xm = nrm(ks[3], (DEPTH, D, MLSTM_WIDTH)) * std
    w_vm = nrm(ks[4], (DEPTH, D, MLSTM_WIDTH)) * std * DEEPNORM_BETA
    w_om = nrm(ks[5], (DEPTH, D, MLSTM_WIDTH)) * std
    w_gt = nrm(ks[6], (DEPTH, D, 2 * H)) * std * 0.1
    w_in = jnp.concatenate([w_qk, w_va, w_xm, w_vm, w_om, w_gt], axis=-1)
    conv_w = nrm(ks[7], (DEPTH, MLSTM_CONV, MLSTM_WIDTH)) * (MLSTM_CONV ** -0.5)
    conv_b = nrm(ks[8], (DEPTH, MLSTM_WIDTH)) * 0.02
    w_mq = nrm(ks[9], (DEPTH, H, E, E)) * (E ** -0.5)
    w_mk = nrm(ks[10], (DEPTH, H, E, E)) * (E ** -0.5)
    b_igate = nrm(ks[11], (DEPTH, H)) * 0.1
    b_fgate = jnp.linspace(3.0, 6.0, H)[None, :] + nrm(ks[12], (DEPTH, H)) * 0.1
    mh_norm_g = 1.0 + 0.02 * nrm(ks[13], (DEPTH, MLSTM_WIDTH))
    w_out = nrm(ks[14], (DEPTH, MIX_WIDTH, D)) * (MIX_WIDTH ** -0.5) * DEEPNORM_BETA
    ln1_g = 1.0 + 0.02 * nrm(ks[15], (DEPTH, D))
    ln1_b = 0.02 * nrm(ks[16], (DEPTH, D))
    peer_w_query = nrm(ks[17], (DEPTH, D, PEER_HEADS * PEER_KEY_DIM)) * std
    peer_sub_keys = nrm(ks[18], (DEPTH, PEER_HEADS, 2, PEER_N_KEYS, PEER_KEY_DIM // 2)) * ((PEER_KEY_DIM // 2) ** -0.5)
    peer_w_down = nrm(ks[19], (DEPTH, PEER_N_EXPERTS, D)) * std
    peer_w_up = nrm(ks[20], (DEPTH, PEER_N_EXPERTS, D)) * DEEPNORM_BETA
    ln2_g = 1.0 + 0.02 * nrm(ks[21], (DEPTH, D))
    ln2_b = 0.02 * nrm(ks[22], (DEPTH, D))
    return {"x": x, "w_in": w_in, "conv_w": conv_w, "conv_b": conv_b, "w_mq": w_mq, "w_mk": w_mk,
            "b_igate": b_igate, "b_fgate": b_fgate, "mh_norm_g": mh_norm_g, "w_out": w_out,
            "ln1_g": ln1_g, "ln1_b": ln1_b, "peer_w_query": peer_w_query, "peer_sub_keys": peer_sub_keys,
            "peer_w_down": peer_w_down, "peer_w_up": peer_w_up, "ln2_g": ln2_g, "ln2_b": ln2_b}


def reference(x, w_in, conv_w, conv_b, w_mq, w_mk, b_igate, b_fgate, mh_norm_g, w_out, ln1_g, ln1_b,
              peer_w_query, peer_sub_keys, peer_w_down, peer_w_up, ln2_g, ln2_b):
    h = x
    for l in range(DEPTH):
        mix = _token_mixing(h, w_in[l], conv_w[l], conv_b[l], w_mq[l], w_mk[l], b_igate[l], b_fgate[l],
                            mh_norm_g[l], w_out[l])
        h = _layer_norm(DEEPNORM_ALPHA * h + mix, ln1_g[l], ln1_b[l])
        ffn = _peer(h, peer_w_query[l], peer_sub_keys[l], peer_w_down[l], peer_w_up[l])
        h = _layer_norm(DEEPNORM_ALPHA * h + ffn, ln2_g[l], ln2_b[l])
    return h
```

```python
import functools
import math

import numpy as np
import jax
import jax.numpy as jnp
from jax import lax
from jax.experimental import pallas as pl
from jax.experimental.pallas import tpu as pltpu

F32 = jnp.float32
BF16 = jnp.bfloat16

ATTN_HEAD_DIM = 64
DILATED_PATTERNS = ((128, 1), (512, 4), (2048, 16))
MLSTM_HEADS = 4
MLSTM_CONV = 4
MLSTM_CHUNK = 128
PEER_HEADS = 8
PEER_N_KEYS = 128
PEER_TOPK = 16
DEPTH = 1
DEEPNORM_ALPHA = (2.0 * DEPTH) ** 0.25
LN_EPS = 1e-5

LANES = 128
NEG_BIG = -1e30
VMEM_LIMIT = 56 * 1024 * 1024


def _cparams(*sem):
    return pltpu.CompilerParams(dimension_semantics=sem, vmem_limit_bytes=VMEM_LIMIT)


def _in_proj_kernel(x_ref, w_ref, wg_ref, *out_refs):
    xb = x_ref[...].astype(BF16)
    g_ref = out_refs[-1]
    for i, o_ref in enumerate(out_refs[:-1]):
        n = o_ref.shape[1]
        o_ref[...] = jnp.dot(xb, w_ref[:, i * n:(i + 1) * n], preferred_element_type=F32).astype(o_ref.dtype)
    g_ref[...] = jnp.dot(xb, wg_ref[...], preferred_element_type=F32)


def _in_proj(x2, w_main, w_gate, tm):
    T, D = x2.shape
    n_main = w_main.shape[1]
    n_out = 6
    wcol = n_main // n_out
    out_shape = [jax.ShapeDtypeStruct((T, wcol), BF16) for _ in range(n_out)]
    out_shape.append(jax.ShapeDtypeStruct((T, LANES), F32))
    out_specs = [pl.BlockSpec((tm, wcol), lambda i: (i, 0)) for _ in range(n_out)]
    out_specs.append(pl.BlockSpec((tm, LANES), lambda i: (i, 0)))
    return pl.pallas_call(
        _in_proj_kernel,
        grid=(T // tm,),
        in_specs=[pl.BlockSpec((tm, D), lambda i: (i, 0)),
                  pl.BlockSpec((D, n_main), lambda i: (0, 0)),
                  pl.BlockSpec((D, LANES), lambda i: (0, 0))],
        out_specs=out_specs,
        out_shape=out_shape,
        compiler_params=_cparams("parallel"),
        name="in_proj",
    )(x2, w_main, w_gate)


def _attn_bias_table(S, blk):
    nrel = S // blk
    rel = np.arange(nrel)[:, None, None] * blk
    delta = rel + np.arange(blk)[None, :, None] - np.arange(blk)[None, None, :]
    cnt = np.zeros(delta.shape, np.float64)
    for window, dil in DILATED_PATTERNS:
        cnt += (delta >= 0) & (delta <= window) & (delta % dil == 0)
    bias = np.where(cnt > 0, np.log(np.maximum(cnt, 1.0)), NEG_BIG)
    return jnp.asarray(bias, F32)


def _attn_kernel(q_ref, k_ref, v_ref, bias_ref, o_ref, *, blk, scale):
    S = q_ref.shape[0]
    nblk = S // blk
    lane = lax.broadcasted_iota(jnp.int32, (blk, LANES), 1)
    lo = lane < ATTN_HEAD_DIM

    def q_block(i, carry):
        r0 = pl.multiple_of(i * blk, blk)
        q = q_ref[pl.ds(r0, blk), :]
        qs = (q.astype(F32) * scale).astype(BF16)
        zero = jnp.zeros_like(qs)
        q2 = jnp.concatenate([jnp.where(lo, qs, zero), jnp.where(lo, zero, qs)], axis=0)

        def k_block(j, st):
            m, l, acc = st
            c0 = pl.multiple_of(j * blk, blk)
            kt = k_ref[pl.ds(c0, blk), :]
            vt = v_ref[pl.ds(c0, blk), :]
            s = lax.dot_general(q2, kt, (((1,), (1,)), ((), ())), preferred_element_type=F32)
            b = bias_ref[i - j]
            s = s + jnp.concatenate([b, b], axis=0)
            m_new = jnp.maximum(m, jnp.max(s, axis=1, keepdims=True))
            alpha = jnp.exp(m - m_new)
            p = jnp.exp(s - m_new)
            l = alpha * l + jnp.sum(p, axis=1, keepdims=True)
            acc = alpha * acc + jnp.dot(p.astype(BF16), vt, preferred_element_type=F32)
            return m_new, l, acc

        init = (jnp.full((2 * blk, 1), -jnp.inf, F32), jnp.zeros((2 * blk, 1), F32),
                jnp.zeros((2 * blk, LANES), F32))
        m, l, acc = lax.fori_loop(0, i + 1, k_block, init)
        out = acc / l
        o_ref[pl.ds(r0, blk), :] = jnp.where(lo, out[:blk], out[blk:]).astype(o_ref.dtype)
        return carry

    lax.fori_loop(0, nblk, q_block, 0)


def _dilated_attention(q, k, v, B, S):
    T, W = q.shape
    blk = 128
    bias = _attn_bias_table(S, blk)
    npair = W // LANES
    spec = pl.BlockSpec((S, LANES), lambda b, h: (b, h))
    return pl.pallas_call(
        functools.partial(_attn_kernel, blk=blk, scale=ATTN_HEAD_DIM ** -0.5),
        grid=(B, npair),
        in_specs=[spec, spec, spec, pl.BlockSpec(bias.shape, lambda b, h: (0, 0, 0))],
        out_specs=spec,
        out_shape=jax.ShapeDtypeStruct((T, W), BF16),
        compiler_params=_cparams("parallel", "parallel"),
        name="dilated_attn",
    )(q, k, v, bias)


def _log_sigmoid(x):
    return jnp.minimum(x, 0.0) - jnp.log1p(jnp.exp(-jnp.abs(x)))


def _mlstm_kernel(xm_ref, vm_ref, om_ref, gcol_ref, grow_ref, gb_row_ref, gb_col_ref, cw_ref, cb_ref,
                  wq_ref, wk_ref, ng_ref, o_ref, q_s, k_s, c_s):
    S, W = xm_ref.shape
    H = MLSTM_HEADS
    E = W // H
    CH = MLSTM_CHUNK
    NC = S // CH
    hi = lax.Precision.HIGHEST

    x32 = xm_ref[...].astype(F32)
    row = lax.broadcasted_iota(jnp.int32, (S, W), 0)
    xc = x32 * cw_ref[MLSTM_CONV - 1:MLSTM_CONV, :] + cb_ref[...]
    for j in range(1, MLSTM_CONV):
        sh = jnp.where(row >= j, pltpu.roll(x32, j, axis=0), 0.0)
        xc = xc + sh * cw_ref[MLSTM_CONV - 1 - j:MLSTM_CONV - j, :]
    xc = (xc * jax.nn.sigmoid(xc)).astype(BF16)
    for h in range(H):
        xh = xc[:, h * E:(h + 1) * E]
        q_s[h] = (jnp.dot(xh, wq_ref[h], preferred_element_type=F32) * (E ** -0.5)).astype(BF16)
        k_s[h] = jnp.dot(xh, wk_ref[h], preferred_element_type=F32).astype(BF16)

    c_s[...] = jnp.zeros_like(c_s)
    ri = lax.broadcasted_iota(jnp.int32, (CH, CH), 0)
    ci = lax.broadcasted_iota(jnp.int32, (CH, CH), 1)
    tril = ri >= ci
    tril_f = tril.astype(F32)
    triu_f = (ri <= ci).astype(F32)

    def chunk(c, carry):
        r0 = pl.multiple_of(c * CH, CH)
        gc = gcol_ref[c] + gb_row_ref[...]
        gr = grow_ref[c] + gb_col_ref[...]
        bc_col = jnp.dot(tril_f, _log_sigmoid(gc), precision=hi, preferred_element_type=F32)
        bc_row = jnp.dot(_log_sigmoid(gr), triu_f, precision=hi, preferred_element_type=F32)
        new_carry = []
        for h in range(H):
            n_prev, m_prev = carry[h]
            bcol = bc_col[:, H + h:H + h + 1]
            brow = bc_row[H + h:H + h + 1, :]
            icol = gc[:, h:h + 1]
            irow = gr[h:h + 1, :]
            qc = q_s[h, pl.ds(r0, CH), :]
            kc = k_s[h, pl.ds(r0, CH), :]
            vc = vm_ref[pl.ds(r0, CH), h * E:(h + 1) * E]
            C = c_s[h]

            dlog = jnp.where(tril, bcol - brow + irow, -jnp.inf)
            m_inter = bcol + m_prev
            m_t = jnp.maximum(m_inter, jnp.max(dlog, axis=1, keepdims=True))
            w_inter = jnp.exp(m_inter - m_t)
            qk = lax.dot_general(qc, kc, (((1,), (1,)), ((), ())), preferred_element_type=F32)
            P = jnp.exp(dlog - m_t) * qk
            inter = lax.dot_general(qc, C.astype(BF16), (((1,), (1,)), ((), ())), preferred_element_type=F32)
            num = w_inter * inter + jnp.dot(P.astype(BF16), vc, preferred_element_type=F32)
            qn = jnp.sum(qc.astype(F32) * n_prev, axis=1, keepdims=True)
            den = w_inter * qn + jnp.sum(P, axis=1, keepdims=True)
            ht = num / jnp.maximum(jnp.abs(den), jnp.exp(-m_t))

            bL = bcol[CH - 1:CH, :]
            g_row = bL - brow + irow
            g_col = bL - bcol + icol
            m_new = jnp.maximum(bL + m_prev, jnp.max(g_row, axis=1, keepdims=True))
            a = jnp.exp(bL + m_prev - m_new)
            wk_col = jnp.exp(g_col - m_new)
            vw = (vc.astype(F32) * wk_col).astype(BF16)
            c_s[h] = a * C + lax.dot_general(vw, kc, (((0,), (0,)), ((), ())), preferred_element_type=F32)
            n_new = a * n_prev + jnp.sum(kc.astype(F32) * wk_col, axis=0, keepdims=True)
            new_carry.append((n_new, m_new))

            mu = jnp.mean(ht, axis=1, keepdims=True)
            var = jnp.mean(jnp.square(ht - mu), axis=1, keepdims=True)
            hn = (ht - mu) * lax.rsqrt(var + LN_EPS) * ng_ref[:, h * E:(h + 1) * E]
            og = om_ref[pl.ds(r0, CH), h * E:(h + 1) * E].astype(F32)
            o_ref[pl.ds(r0, CH), h * E:(h + 1) * E] = (jax.nn.sigmoid(og) * hn).astype(o_ref.dtype)
        return tuple(new_carry)

    init = tuple((jnp.zeros((1, E), F32), jnp.zeros((1, 1), F32)) for _ in range(H))
    lax.fori_loop(0, NC, chunk, init)


def _mlstm(xm, vm, om, gates, conv_w, conv_b, w_mq, w_mk, b_igate, b_fgate, mh_norm_g, B, S):
    T, W = xm.shape
    H = MLSTM_HEADS
    E = W // H
    CH = MLSTM_CHUNK
    NC = S // CH
    gcol = gates.reshape(B, NC, CH, 2 * H)
    grow = gcol.transpose(0, 1, 3, 2)
    gb = jnp.concatenate([b_igate, b_fgate]).astype(F32)
    seq = pl.BlockSpec((S, W), lambda b: (b, 0))
    full2 = lambda shape: pl.BlockSpec(shape, lambda b: (0,) * len(shape))
    return pl.pallas_call(
        _mlstm_kernel,
        grid=(B,),
        in_specs=[seq, seq, seq,
                  pl.BlockSpec((None, NC, CH, 2 * H), lambda b: (b, 0, 0, 0)),
                  pl.BlockSpec((None, NC, 2 * H, CH), lambda b: (b, 0, 0, 0)),
                  full2((1, 2 * H)), full2((2 * H, 1)), full2((MLSTM_CONV, W)), full2((1, W)),
                  full2((H, E, E)), full2((H, E, E)), full2((1, W))],
        out_specs=seq,
        out_shape=jax.ShapeDtypeStruct((T, W), BF16),
        scratch_shapes=[pltpu.VMEM((H, S, E), BF16), pltpu.VMEM((H, S, E), BF16), pltpu.VMEM((H, E, E), F32)],
        compiler_params=_cparams("parallel"),
        name="mlstm",
    )(xm, vm, om, gcol, grow, gb[None, :], gb[:, None], conv_w.astype(F32), conv_b[None, :].astype(F32),
      w_mq.astype(BF16), w_mk.astype(BF16), mh_norm_g[None, :].astype(F32))


def _layer_norm_rows(y, g, b):
    mu = jnp.mean(y, axis=1, keepdims=True)
    d = y - mu
    var = jnp.mean(d * d, axis=1, keepdims=True)
    return d * lax.rsqrt(var + LN_EPS) * g + b


def _out_proj_kernel(x_ref, a_ref, m_ref, wa_ref, wm_ref, g_ref, b_ref, o_ref):
    mix = jnp.dot(a_ref[...], wa_ref[...], preferred_element_type=F32)
    mix = mix + jnp.dot(m_ref[...], wm_ref[...], preferred_element_type=F32)
    o_ref[...] = _layer_norm_rows(DEEPNORM_ALPHA * x_ref[...] + mix, g_ref[...], b_ref[...])


def _out_proj_ln(x2, attn, mlstm, w_out, g, b, tm):
    T, D = x2.shape
    Wa = attn.shape[1]
    Wm = mlstm.shape[1]
    wa = w_out[:Wa].astype(BF16)
    wm = w_out[Wa:].astype(BF16)
    return pl.pallas_call(
        _out_proj_kernel,
        grid=(T // tm,),
        in_specs=[pl.BlockSpec((tm, D), lambda i: (i, 0)),
                  pl.BlockSpec((tm, Wa), lambda i: (i, 0)),
                  pl.BlockSpec((tm, Wm), lambda i: (i, 0)),
                  pl.BlockSpec((Wa, D), lambda i: (0, 0)),
                  pl.BlockSpec((Wm, D), lambda i: (0, 0)),
                  pl.BlockSpec((1, D), lambda i: (0, 0)),
                  pl.BlockSpec((1, D), lambda i: (0, 0))],
        out_specs=pl.BlockSpec((tm, D), lambda i: (i, 0)),
        out_shape=jax.ShapeDtypeStruct((T, D), F32),
        compiler_params=_cparams("parallel"),
        name="out_proj_ln",
    )(x2, attn, mlstm, wa, wm, g[None, :].astype(F32), b[None, :].astype(F32))


def _topk_rows(s, k, payload=None):
    R, N = s.shape
    rid = lax.broadcasted_iota(jnp.int32, (R, N), 0)
    vals, idxs = [], []
    for _ in range(k):
        mx = jnp.max(s, axis=0, keepdims=True)
        ix = jnp.min(jnp.where(s == mx, rid, R), axis=0, keepdims=True)
        sel = rid == ix
        vals.append(mx)
        if payload is None:
            idxs.append(ix)
        else:
            idxs.append(jnp.sum(jnp.where(sel, payload, 0), axis=0, keepdims=True))
        s = jnp.where(sel, -jnp.inf, s)
    return jnp.concatenate(vals, axis=0), jnp.concatenate(idxs, axis=0)


def _peer_route_kernel(h_ref, wq_ref, keys_ref, eidx_ref, gate_ref, *, sub):
    tm = h_ref.shape[0]
    K = PEER_TOPK
    hb = h_ref[...].astype(BF16)
    qry = jnp.dot(hb, wq_ref[...], preferred_element_type=F32).astype(BF16)
    pairs = [(a, b) for a in range(K) for b in range(K) if (a + 1) * (b + 1) <= K]
    for t0 in range(0, tm, sub):
        e_rows, g_rows = [], []
        for h in range(PEER_HEADS):
            tops = []
            for p in range(2):
                c0 = (h * 2 + p) * LANES
                qhp = qry[t0:t0 + sub, c0:c0 + LANES]
                sc = lax.dot_general(keys_ref[h, p], qhp, (((1,), (1,)), ((), ())),
                                     preferred_element_type=F32)
                tops.append(_topk_rows(sc, K))
            (v1, i1), (v2, i2) = tops
            cand = jnp.concatenate([v1[a:a + 1] + v2[b:b + 1] for a, b in pairs], axis=0)
            cidx = jnp.concatenate([i1[a:a + 1] * PEER_N_KEYS + i2[b:b + 1] for a, b in pairs], axis=0)
            best, eidx = _topk_rows(cand, K, payload=cidx)
            ex = jnp.exp(best - best[0:1])
            g_rows.append(ex / jnp.sum(ex, axis=0, keepdims=True))
            e_rows.append(eidx)
        eidx_ref[:, t0:t0 + sub] = jnp.concatenate(e_rows, axis=0)
        gate_ref[:, t0:t0 + sub] = jnp.concatenate(g_rows, axis=0)


def _peer_route(h1, w_query, sub_keys, tm):
    T, D = h1.shape
    NQ = w_query.shape[1]
    R = PEER_HEADS * PEER_TOPK
    return pl.pallas_call(
        functools.partial(_peer_route_kernel, sub=LANES),
        grid=(T // tm,),
        in_specs=[pl.BlockSpec((tm, D), lambda i: (i, 0)),
                  pl.BlockSpec((D, NQ), lambda i: (0, 0)),
                  pl.BlockSpec(sub_keys.shape, lambda i: (0, 0, 0, 0))],
        out_specs=[pl.BlockSpec((R, tm), lambda i: (0, i)), pl.BlockSpec((R, tm), lambda i: (0, i))],
        out_shape=[jax.ShapeDtypeStruct((R, T), jnp.int32), jax.ShapeDtypeStruct((R, T), F32)],
        compiler_params=_cparams("parallel"),
        name="peer_route",
    )(h1, w_query.astype(BF16), sub_keys.astype(BF16))


PEER_TOKENS_PER_STEP = 32
PEER_RING = 4


def _gelu_exact(x):
    return 0.5 * x * (1.0 + lax.erf(x * (2.0 ** -0.5)))


def _peer_expert_kernel(idx_ref, h_ref, gate_ref, g_ref, b_ref, table_ref, o_ref, buf, sem, acc, *, nsteps):
    tb, D = h_ref.shape
    R = buf.shape[1]
    i = pl.program_id(0)

    def issue(row, slot):
        for k in range(R):
            pltpu.make_async_copy(table_ref.at[idx_ref[0, row, k]], buf.at[slot, k], sem.at[slot]).start()

    def wait(slot):
        pltpu.make_async_copy(table_ref.at[pl.ds(0, R)], buf.at[slot], sem.at[slot]).wait()

    @pl.when(i == 0)
    def _():
        for s in range(PEER_RING):
            issue(s, s)

    lane_tok = lax.broadcasted_iota(jnp.int32, (1, tb), 1)

    def group(jj, carry):
        for s in range(PEER_RING):
            j = jj * PEER_RING + s
            wait(s)
            x = h_ref[pl.ds(j, 1), :]
            a = jnp.sum(buf[s, :, :D] * x, axis=1, keepdims=True)
            gcol = jnp.sum(jnp.where(lane_tok == j, gate_ref[...], 0.0), axis=1, keepdims=True)
            coef = _gelu_exact(a) * gcol
            acc[pl.ds(j, 1), :] = jnp.sum(buf[s, :, D:] * coef, axis=0, keepdims=True)

            @pl.when((j + PEER_RING < tb) | (i + 1 < nsteps))
            def _():
                issue(j + PEER_RING, s)
        return carry

    lax.fori_loop(0, tb // PEER_RING, group, 0)
    o_ref[...] = _layer_norm_rows(DEEPNORM_ALPHA * h_ref[...] + acc[...], g_ref[...], b_ref[...])


def _peer_experts(h1, eidx_t, gate_t, table, g, b):
    T, D = h1.shape
    R = eidx_t.shape[0]
    tb = PEER_TOKENS_PER_STEP
    nsteps = T // tb
    eidx = jnp.pad(eidx_t.T, ((0, tb), (0, 0)))
    win = (jnp.arange(nsteps)[:, None] * tb + jnp.arange(tb + PEER_RING)[None, :])
    eidx_win = eidx[win]
    gate_blk = gate_t.reshape(R, nsteps, tb).transpose(1, 0, 2)
    return pl.pallas_call(
        functools.partial(_peer_expert_kernel, nsteps=nsteps),
        grid=(nsteps,),
        in_specs=[pl.BlockSpec((1, tb + PEER_RING, R), lambda i: (i, 0, 0), memory_space=pltpu.SMEM),
                  pl.BlockSpec((tb, D), lambda i: (i, 0)),
                  pl.BlockSpec((None, R, tb), lambda i: (i, 0, 0)),
                  pl.BlockSpec((1, D), lambda i: (0, 0)),
                  pl.BlockSpec((1, D), lambda i: (0, 0)),
                  pl.BlockSpec(memory_space=pl.ANY)],
        out_specs=pl.BlockSpec((tb, D), lambda i: (i, 0)),
        out_shape=jax.ShapeDtypeStruct((T, D), F32),
        scratch_shapes=[pltpu.VMEM((PEER_RING, R, 2 * D), F32),
                        pltpu.SemaphoreType.DMA((PEER_RING,)),
                        pltpu.VMEM((tb, D), F32)],
        compiler_params=_cparams("arbitrary"),
        name="peer_experts",
    )(eidx_win, h1, gate_blk, g[None, :].astype(F32), b[None, :].astype(F32), table)


def kernel(x, w_in, conv_w, conv_b, w_mq, w_mk, b_igate, b_fgate, mh_norm_g, w_out, ln1_g, ln1_b,
           peer_w_query, peer_sub_keys, peer_w_down, peer_w_up, ln2_g, ln2_b):
    B, S, D = x.shape
    T = B * S
    H = MLSTM_HEADS
    h = x.reshape(T, D)
    for l in range(DEPTH):
        n_gate = 2 * H
        w_main = w_in[l][:, :-n_gate].astype(BF16)
        w_gate = jnp.pad(w_in[l][:, -n_gate:], ((0, 0), (0, LANES - n_gate))).astype(BF16)
        q_a, k_a, v_a, x_m, v_m, o_m, gates = _in_proj(h, w_main, w_gate, tm=min(512, T))
        attn = _dilated_attention(q_a, k_a, v_a, B, S)
        mlstm = _mlstm(x_m, v_m, o_m, gates[:, :n_gate], conv_w[l], conv_b[l], w_mq[l], w_mk[l],
                       b_igate[l], b_fgate[l], mh_norm_g[l], B, S)
        h1 = _out_proj_ln(h, attn, mlstm, w_out[l], ln1_g[l], ln1_b[l], tm=min(512, T))
        eidx_t, gate_t = _peer_route(h1, peer_w_query[l], peer_sub_keys[l], tm=min(512, T))
        table = jnp.concatenate([peer_w_down[l], peer_w_up[l]], axis=1)
        h = _peer_experts(h1, eidx_t, gate_t, table, ln2_g[l], ln2_b[l])
    return h.reshape(B, S, D)
```

```python
import functools
import math

import numpy as np
import jax
import jax.numpy as jnp
from jax import lax
from jax.experimental import pallas as pl
from jax.experimental.pallas import tpu as pltpu

F32 = jnp.float32
BF16 = jnp.bfloat16

ATTN_HEAD_DIM = 64
DILATED_PATTERNS = ((128, 1), (512, 4), (2048, 16))
MLSTM_HEADS = 4
MLSTM_CONV = 4
MLSTM_CHUNK = 128
PEER_HEADS = 8
PEER_N_KEYS = 128
PEER_TOPK = 16
DEPTH = 1
DEEPNORM_ALPHA = (2.0 * DEPTH) ** 0.25
LN_EPS = 1e-5

LANES = 128
NEG_BIG = -1e30
VMEM_LIMIT = 56 * 1024 * 1024


def _cparams(*sem):
    return pltpu.CompilerParams(dimension_semantics=sem, vmem_limit_bytes=VMEM_LIMIT)


def _in_proj_kernel(x_ref, w_ref, wg_ref, *out_refs):
    xb = x_ref[...].astype(BF16)
    g_ref = out_refs[-1]
    for i, o_ref in enumerate(out_refs[:-1]):
        n = o_ref.shape[1]
        o_ref[...] = jnp.dot(xb, w_ref[:, i * n:(i + 1) * n], preferred_element_type=F32).astype(o_ref.dtype)
    g_ref[...] = jnp.dot(xb, wg_ref[...], preferred_element_type=F32)


def _in_proj(x2, w_main, w_gate, tm):
    T, D = x2.shape
    n_main = w_main.shape[1]
    n_out = 6
    wcol = n_main // n_out
    out_shape = [jax.ShapeDtypeStruct((T, wcol), BF16) for _ in range(n_out)]
    out_shape.append(jax.ShapeDtypeStruct((T, LANES), F32))
    out_specs = [pl.BlockSpec((tm, wcol), lambda i: (i, 0)) for _ in range(n_out)]
    out_specs.append(pl.BlockSpec((tm, LANES), lambda i: (i, 0)))
    return pl.pallas_call(
        _in_proj_kernel,
        grid=(T // tm,),
        in_specs=[pl.BlockSpec((tm, D), lambda i: (i, 0)),
                  pl.BlockSpec((D, n_main), lambda i: (0, 0)),
                  pl.BlockSpec((D, LANES), lambda i: (0, 0))],
        out_specs=out_specs,
        out_shape=out_shape,
        compiler_params=_cparams("parallel"),
        name="in_proj",
    )(x2, w_main, w_gate)


def _attn_bias_table(S, blk):
    nrel = S // blk
    rel = np.arange(nrel)[:, None, None] * blk
    delta = rel + np.arange(blk)[None, :, None] - np.arange(blk)[None, None, :]
    cnt = np.zeros(delta.shape, np.float64)
    for window, dil in DILATED_PATTERNS:
        cnt += (delta >= 0) & (delta <= window) & (delta % dil == 0)
    bias = np.where(cnt > 0, np.log(np.maximum(cnt, 1.0)), NEG_BIG)
    return jnp.asarray(bias, F32)


def _attn_kernel(q_ref, k_ref, v_ref, bias_ref, o_ref, *, blk, scale):
    S = q_ref.shape[0]
    nblk = S // blk
    lane = lax.broadcasted_iota(jnp.int32, (blk, LANES), 1)
    lo = lane < ATTN_HEAD_DIM

    def q_block(i, carry):
        r0 = pl.multiple_of(i * blk, blk)
        q = q_ref[pl.ds(r0, blk), :]
        qs = (q.astype(F32) * scale).astype(BF16)
        zero = jnp.zeros_like(qs)
        q2 = jnp.concatenate([jnp.where(lo, qs, zero), jnp.where(lo, zero, qs)], axis=0)

        def k_block(j, st):
            m, l, acc = st
            c0 = pl.multiple_of(j * blk, blk)
            kt = k_ref[pl.ds(c0, blk), :]
            vt = v_ref[pl.ds(c0, blk), :]
            s = lax.dot_general(q2, kt, (((1,), (1,)), ((), ())), preferred_element_type=F32)
            b = bias_ref[i - j]
            s = s + jnp.concatenate([b, b], axis=0)
            m_new = jnp.maximum(m, jnp.max(s, axis=1, keepdims=True))
            alpha = jnp.exp(m - m_new)
            p = jnp.exp(s - m_new)
            l = alpha * l + jnp.sum(p, axis=1, keepdims=True)
            acc = alpha * acc + jnp.dot(p.astype(BF16), vt, preferred_element_type=F32)
            return m_new, l, acc

        init = (jnp.full((2 * blk, 1), -jnp.inf, F32), jnp.zeros((2 * blk, 1), F32),
                jnp.zeros((2 * blk, LANES), F32))
        m, l, acc = lax.fori_loop(0, i + 1, k_block, init)
        out = acc / l
        o_ref[pl.ds(r0, blk), :] = jnp.where(lo, out[:blk], out[blk:]).astype(o_ref.dtype)
        return carry

    lax.fori_loop(0, nblk, q_block, 0)


def _dilated_attention(q, k, v, B, S):
    T, W = q.shape
    blk = 128
    bias = _attn_bias_table(S, blk)
    npair = W // LANES
    spec = pl.BlockSpec((S, LANES), lambda b, h: (b, h))
    return pl.pallas_call(
        functools.partial(_attn_kernel, blk=blk, scale=ATTN_HEAD_DIM ** -0.5),
        grid=(B, npair),
        in_specs=[spec, spec, spec, pl.BlockSpec(bias.shape, lambda b, h: (0, 0, 0))],
        out_specs=spec,
        out_shape=jax.ShapeDtypeStruct((T, W), BF16),
        compiler_params=_cparams("parallel", "parallel"),
        name="dilated_attn",
    )(q, k, v, bias)


def _log_sigmoid(x):
    return jnp.minimum(x, 0.0) - jnp.log1p(jnp.exp(-jnp.abs(x)))


def _mlstm_kernel(xm_ref, vm_ref, om_ref, gcol_ref, grow_ref, gb_row_ref, gb_col_ref, cw_ref, cb_ref,
                  wq_ref, wk_ref, ng_ref, o_ref, q_s, k_s, c_s):
    S, W = xm_ref.shape
    H = MLSTM_HEADS
    E = W // H
    CH = MLSTM_CHUNK
    NC = S // CH
    hi = lax.Precision.HIGHEST

    x32 = xm_ref[...].astype(F32)
    row = lax.broadcasted_iota(jnp.int32, (S, W), 0)
    xc = x32 * cw_ref[MLSTM_CONV - 1:MLSTM_CONV, :] + cb_ref[...]
    for j in range(1, MLSTM_CONV):
        sh = jnp.where(row >= j, pltpu.roll(x32, j, axis=0), 0.0)
        xc = xc + sh * cw_ref[MLSTM_CONV - 1 - j:MLSTM_CONV - j, :]
    xc = (xc * jax.nn.sigmoid(xc)).astype(BF16)
    for h in range(H):
        xh = xc[:, h * E:(h + 1) * E]
        q_s[h] = (jnp.dot(xh, wq_ref[h], preferred_element_type=F32) * (E ** -0.5)).astype(BF16)
        k_s[h] = jnp.dot(xh, wk_ref[h], preferred_element_type=F32).astype(BF16)

    c_s[...] = jnp.zeros_like(c_s)
    ri = lax.broadcasted_iota(jnp.int32, (CH, CH), 0)
    ci = lax.broadcasted_iota(jnp.int32, (CH, CH), 1)
    tril = ri >= ci
    tril_f = tril.astype(F32)
    triu_f = (ri <= ci).astype(F32)

    def chunk(c, carry):
        r0 = pl.multiple_of(c * CH, CH)
        gc = gcol_ref[c] + gb_row_ref[...]
        gr = grow_ref[c] + gb_col_ref[...]
        bc_col = jnp.dot(tril_f, _log_sigmoid(gc), precision=hi, preferred_element_type=F32)
        bc_row = jnp.dot(_log_sigmoid(gr), triu_f, precision=hi, preferred_element_type=F32)
        new_carry = []
        for h in range(H):
            n_prev, m_prev = carry[h]
            bcol = bc_col[:, H + h:H + h + 1]
            brow = bc_row[H + h:H + h + 1, :]
            icol = gc[:, h:h + 1]
            irow = gr[h:h + 1, :]
            qc = q_s[h, pl.ds(r0, CH), :]
            kc = k_s[h, pl.ds(r0, CH), :]
            vc = vm_ref[pl.ds(r0, CH), h * E:(h + 1) * E]
            C = c_s[h]

            dlog = jnp.where(tril, bcol - brow + irow, -jnp.inf)
            m_inter = bcol + m_prev
            m_t = jnp.maximum(m_inter, jnp.max(dlog, axis=1, keepdims=True))
            w_inter = jnp.exp(m_inter - m_t)
            qk = lax.dot_general(qc, kc, (((1,), (1,)), ((), ())), preferred_element_type=F32)
            P = jnp.exp(dlog - m_t) * qk
            inter = lax.dot_general(qc, C.astype(BF16), (((1,), (1,)), ((), ())), preferred_element_type=F32)
            num = w_inter * inter + jnp.dot(P.astype(BF16), vc, preferred_element_type=F32)
            qn = jnp.sum(qc.astype(F32) * n_prev, axis=1, keepdims=True)
            den = w_inter * qn + jnp.sum(P, axis=1, keepdims=True)
            ht = num / jnp.maximum(jnp.abs(den), jnp.exp(-m_t))

            bL = bcol[CH - 1:CH, :]
            g_row = bL - brow + irow
            g_col = bL - bcol + icol
            m_new = jnp.maximum(bL + m_prev, jnp.max(g_row, axis=1, keepdims=True))
            a = jnp.exp(bL + m_prev - m_new)
            wk_col = jnp.exp(g_col - m_new)
            vw = (vc.astype(F32) * wk_col).astype(BF16)
            c_s[h] = a * C + lax.dot_general(vw, kc, (((0,), (0,)), ((), ())), preferred_element_type=F32)
            n_new = a * n_prev + jnp.sum(kc.astype(F32) * wk_col, axis=0, keepdims=True)
            new_carry.append((n_new, m_new))

            mu = jnp.mean(ht, axis=1, keepdims=True)
            var = jnp.mean(jnp.square(ht - mu), axis=1, keepdims=True)
            hn = (ht - mu) * lax.rsqrt(var + LN_EPS) * ng_ref[:, h * E:(h + 1) * E]
            og = om_ref[pl.ds(r0, CH), h * E:(h + 1) * E].astype(F32)
            o_ref[pl.ds(r0, CH), h * E:(h + 1) * E] = (jax.nn.sigmoid(og) * hn).astype(o_ref.dtype)
        return tuple(new_carry)

    init = tuple((jnp.zeros((1, E), F32), jnp.zeros((1, 1), F32)) for _ in range(H))
    lax.fori_loop(0, NC, chunk, init)


def _mlstm(xm, vm, om, gates, conv_w, conv_b, w_mq, w_mk, b_igate, b_fgate, mh_norm_g, B, S):
    T, W = xm.shape
    H = MLSTM_HEADS
    E = W // H
    CH = MLSTM_CHUNK
    NC = S // CH
    gcol = gates.reshape(B, NC, CH, 2 * H)
    grow = gcol.transpose(0, 1, 3, 2)
    gb = jnp.concatenate([b_igate, b_fgate]).astype(F32)
    seq = pl.BlockSpec((S, W), lambda b: (b, 0))
    full2 = lambda shape: pl.BlockSpec(shape, lambda b: (0,) * len(shape))
    return pl.pallas_call(
        _mlstm_kernel,
        grid=(B,),
        in_specs=[seq, seq, seq,
                  pl.BlockSpec((None, NC, CH, 2 * H), lambda b: (b, 0, 0, 0)),
                  pl.BlockSpec((None, NC, 2 * H, CH), lambda b: (b, 0, 0, 0)),
                  full2((1, 2 * H)), full2((2 * H, 1)), full2((MLSTM_CONV, W)), full2((1, W)),
                  full2((H, E, E)), full2((H, E, E)), full2((1, W))],
        out_specs=seq,
        out_shape=jax.ShapeDtypeStruct((T, W), BF16),
        scratch_shapes=[pltpu.VMEM((H, S, E), BF16), pltpu.VMEM((H, S, E), BF16), pltpu.VMEM((H, E, E), F32)],
        compiler_params=_cparams("parallel"),
        name="mlstm",
    )(xm, vm, om, gcol, grow, gb[None, :], gb[:, None], conv_w.astype(F32), conv_b[None, :].astype(F32),
      w_mq.astype(BF16), w_mk.astype(BF16), mh_norm_g[None, :].astype(F32))


def _layer_norm_rows(y, g, b):
    mu = jnp.mean(y, axis=1, keepdims=True)
    d = y - mu
    var = jnp.mean(d * d, axis=1, keepdims=True)
    return d * lax.rsqrt(var + LN_EPS) * g + b


def _out_proj_kernel(x_ref, a_ref, m_ref, wa_ref, wm_ref, g_ref, b_ref, o_ref):
    mix = jnp.dot(a_ref[...], wa_ref[...], preferred_element_type=F32)
    mix = mix + jnp.dot(m_ref[...], wm_ref[...], preferred_element_type=F32)
    o_ref[...] = _layer_norm_rows(DEEPNORM_ALPHA * x_ref[...] + mix, g_ref[...], b_ref[...])


def _out_proj_ln(x2, attn, mlstm, w_out, g, b, tm):
    T, D = x2.shape
    Wa = attn.shape[1]
    Wm = mlstm.shape[1]
    wa = w_out[:Wa].astype(BF16)
    wm = w_out[Wa:].astype(BF16)
    return pl.pallas_call(
        _out_proj_kernel,
        grid=(T // tm,),
        in_specs=[pl.BlockSpec((tm, D), lambda i: (i, 0)),
                  pl.BlockSpec((tm, Wa), lambda i: (i, 0)),
                  pl.BlockSpec((tm, Wm), lambda i: (i, 0)),
                  pl.BlockSpec((Wa, D), lambda i: (0, 0)),
                  pl.BlockSpec((Wm, D), lambda i: (0, 0)),
                  pl.BlockSpec((1, D), lambda i: (0, 0)),
                  pl.BlockSpec((1, D), lambda i: (0, 0))],
        out_specs=pl.BlockSpec((tm, D), lambda i: (i, 0)),
        out_shape=jax.ShapeDtypeStruct((T, D), F32),
        compiler_params=_cparams("parallel"),
        name="out_proj_ln",
    )(x2, attn, mlstm, wa, wm, g[None, :].astype(F32), b[None, :].astype(F32))


def _topk_rows(s, k, payload=None):
    R, N = s.shape
    rid = lax.broadcasted_iota(jnp.int32, (R, N), 0)
    vals, idxs = [], []
    for _ in range(k):
        mx = jnp.max(s, axis=0, keepdims=True)
        ix = jnp.min(jnp.where(s == mx, rid, R), axis=0, keepdims=True)
        sel = rid == ix
        vals.append(mx)
        if payload is None:
            idxs.append(ix)
        else:
            idxs.append(jnp.sum(jnp.where(sel, payload, 0), axis=0, keepdims=True))
        s = jnp.where(sel, -jnp.inf, s)
    return jnp.concatenate(vals, axis=0), jnp.concatenate(idxs, axis=0)


def _peer_route_kernel(h_ref, wq_ref, keys_ref, eidx_ref, gate_ref, *, sub):
    tm = h_ref.shape[0]
    K = PEER_TOPK
    hb = h_ref[...].astype(BF16)
    qry = jnp.dot(hb, wq_ref[...], preferred_element_type=F32).astype(BF16)
    pairs = [(a, b) for a in range(K) for b in range(K) if (a + 1) * (b + 1) <= K]
    for t0 in range(0, tm, sub):
        e_rows, g_rows = [], []
        for h in range(PEER_HEADS):
            tops = []
            for p in range(2):
                c0 = (h * 2 + p) * LANES
                qhp = qry[t0:t0 + sub, c0:c0 + LANES]
                sc = lax.dot_general(keys_ref[h, p], qhp, (((1,), (1,)), ((), ())),
                                     preferred_element_type=F32)
                tops.append(_topk_rows(sc, K))
            (v1, i1), (v2, i2) = tops
            cand = jnp.concatenate([v1[a:a + 1] + v2[b:b + 1] for a, b in pairs], axis=0)
            cidx = jnp.concatenate([i1[a:a + 1] * PEER_N_KEYS + i2[b:b + 1] for a, b in pairs], axis=0)
            best, eidx = _topk_rows(cand, K, payload=cidx)
            ex = jnp.exp(best - best[0:1])
            g_rows.append(ex / jnp.sum(ex, axis=0, keepdims=True))
            e_rows.append(eidx)
        eidx_ref[:, t0:t0 + sub] = jnp.concatenate(e_rows, axis=0)
        gate_ref[:, t0:t0 + sub] = jnp.concatenate(g_rows, axis=0)


def _peer_route(h1, w_query, sub_keys, tm):
    T, D = h1.shape
    NQ = w_query.shape[1]
    R = PEER_HEADS * PEER_TOPK
    return pl.pallas_call(
        functools.partial(_peer_route_kernel, sub=LANES),
        grid=(T // tm,),
        in_specs=[pl.BlockSpec((tm, D), lambda i: (i, 0)),
                  pl.BlockSpec((D, NQ), lambda i: (0, 0)),
                  pl.BlockSpec(sub_keys.shape, lambda i: (0, 0, 0, 0))],
        out_specs=[pl.BlockSpec((R, tm), lambda i: (0, i)), pl.BlockSpec((R, tm), lambda i: (0, i))],
        out_shape=[jax.ShapeDtypeStruct((R, T), jnp.int32), jax.ShapeDtypeStruct((R, T), F32)],
        compiler_params=_cparams("parallel"),
        name="peer_route",
    )(h1, w_query.astype(BF16), sub_keys.astype(BF16))


PEER_TOKENS_PER_STEP = 32
PEER_SLOTS = 8
LANE_TILES = 16


def _gelu_exact(x):
    return 0.5 * x * (1.0 + lax.erf(x * (2.0 ** -0.5)))


def _peer_expert_kernel(idx_ref, h_ref, gate_ref, g_ref, b_ref, table_ref, o_ref, buf, sem, acc, *, nsteps):
    tb, D = h_ref.shape
    R = buf.shape[2]
    half = LANE_TILES // 2
    ahead = PEER_SLOTS - 1
    i = pl.program_id(0)

    def start_row(row, slot, k):
        pltpu.make_async_copy(table_ref.at[idx_ref[0, row, k]], buf.at[slot, :, k, :], sem.at[slot]).start()

    def wait(slot):
        pltpu.make_async_copy(buf.at[slot], buf.at[slot], sem.at[slot]).wait()

    @pl.when(i == 0)
    def _():
        for s in range(ahead):
            for k in range(R):
                start_row(s, s, k)

    lane_tok = lax.broadcasted_iota(jnp.int32, (1, tb), 1)
    SUB = 8
    npiece = R // SUB
    per_piece = 3
    mid = (R - 2 * npiece * per_piece) // 2

    def token(j, s):
        islot = (s + ahead) % PEER_SLOTS
        pending = iter(range(R))

        def start_rows(n):
            for _ in range(n):
                start_row(j + ahead, islot, next(pending))

        x = h_ref[pl.ds(j, 1), :]
        xb = [jnp.broadcast_to(x[:, c * LANES:(c + 1) * LANES], (SUB, LANES)) for c in range(half)]
        parts = []
        for p in range(npiece):
            rows = slice(p * SUB, (p + 1) * SUB)
            t = buf[s, 0, rows, :] * xb[0]
            for c in range(1, half):
                t = t + buf[s, c, rows, :] * xb[c]
            parts.append(t)
            start_rows(per_piece)
        a = jnp.sum(jnp.concatenate(parts, axis=0), axis=1, keepdims=True)
        gcol = jnp.sum(jnp.where(lane_tok == j, gate_ref[...], 0.0), axis=1, keepdims=True)
        start_rows(mid)
        coef = jnp.broadcast_to(_gelu_exact(a) * gcol, (R, LANES))
        start_rows(R - 2 * npiece * per_piece - mid)
        outs = [None] * half
        for p in range(npiece):
            rows = slice(p * SUB, (p + 1) * SUB)
            cb = coef[rows, :]
            for c in range(half):
                t = buf[s, half + c, rows, :] * cb
                outs[c] = t if outs[c] is None else outs[c] + t
            start_rows(per_piece)
        acc[pl.ds(j, 1), :] = jnp.concatenate([jnp.sum(o, axis=0, keepdims=True) for o in outs], axis=1)

    def group(jj, carry):
        for s in range(PEER_SLOTS):
            wait(s)
            token(jj * PEER_SLOTS + s, s)
        return carry

    lax.fori_loop(0, tb // PEER_SLOTS, group, 0)

    @pl.when(i == nsteps - 1)
    def _():
        for s in range(ahead):
            wait(s)

    o_ref[...] = _layer_norm_rows(DEEPNORM_ALPHA * h_ref[...] + acc[...], g_ref[...], b_ref[...])


def _peer_experts(h1, eidx_t, gate_t, w_down, w_up, g, b):
    T, D = h1.shape
    R = eidx_t.shape[0]
    tb = PEER_TOKENS_PER_STEP
    nsteps = T // tb
    ahead = PEER_SLOTS - 1
    assert tb % PEER_SLOTS == 0 and 2 * D == LANE_TILES * LANES
    table = jnp.concatenate([w_down, w_up], axis=1).reshape(-1, LANE_TILES, LANES)
    eidx = jnp.pad(eidx_t.T, ((0, tb), (0, 0)))
    win = (jnp.arange(nsteps)[:, None] * tb + jnp.arange(tb + ahead)[None, :])
    eidx_win = eidx[win]
    gate_blk = gate_t.reshape(R, nsteps, tb).transpose(1, 0, 2)
    return pl.pallas_call(
        functools.partial(_peer_expert_kernel, nsteps=nsteps),
        grid=(nsteps,),
        in_specs=[pl.BlockSpec((1, tb + ahead, R), lambda i: (i, 0, 0), memory_space=pltpu.SMEM),
                  pl.BlockSpec((tb, D), lambda i: (i, 0)),
                  pl.BlockSpec((None, R, tb), lambda i: (i, 0, 0)),
                  pl.BlockSpec((1, D), lambda i: (0, 0)),
                  pl.BlockSpec((1, D), lambda i: (0, 0)),
                  pl.BlockSpec(memory_space=pl.ANY)],
        out_specs=pl.BlockSpec((tb, D), lambda i: (i, 0)),
        out_shape=jax.ShapeDtypeStruct((T, D), F32),
        scratch_shapes=[pltpu.VMEM((PEER_SLOTS, LANE_TILES, R, LANES), F32),
                        pltpu.SemaphoreType.DMA((PEER_SLOTS,)),
                        pltpu.VMEM((tb, D), F32)],
        compiler_params=_cparams("arbitrary"),
        name="peer_experts",
    )(eidx_win, h1, gate_blk, g[None, :].astype(F32), b[None, :].astype(F32), table)


def kernel(x, w_in, conv_w, conv_b, w_mq, w_mk, b_igate, b_fgate, mh_norm_g, w_out, ln1_g, ln1_b,
           peer_w_query, peer_sub_keys, peer_w_down, peer_w_up, ln2_g, ln2_b):
    B, S, D = x.shape
    T = B * S
    H = MLSTM_HEADS
    h = x.reshape(T, D)
    for l in range(DEPTH):
        n_gate = 2 * H
        w_main = w_in[l][:, :-n_gate].astype(BF16)
        w_gate = jnp.pad(w_in[l][:, -n_gate:], ((0, 0), (0, LANES - n_gate))).astype(BF16)
        q_a, k_a, v_a, x_m, v_m, o_m, gates = _in_proj(h, w_main, w_gate, tm=min(512, T))
        attn = _dilated_attention(q_a, k_a, v_a, B, S)
        mlstm = _mlstm(x_m, v_m, o_m, gates[:, :n_gate], conv_w[l], conv_b[l], w_mq[l], w_mk[l],
                       b_igate[l], b_fgate[l], mh_norm_g[l], B, S)
        h1 = _out_proj_ln(h, attn, mlstm, w_out[l], ln1_g[l], ln1_b[l], tm=min(512, T))
        eidx_t, gate_t = _peer_route(h1, peer_w_query[l], peer_sub_keys[l], tm=min(512, T))
        h = _peer_experts(h1, eidx_t, gate_t, peer_w_down[l], peer_w_up[l], ln2_g[l], ln2_b[l])
    return h.reshape(B, S, D)
```

```python
import functools
import math

import numpy as np
import jax
import jax.numpy as jnp
from jax import lax
from jax.experimental import pallas as pl
from jax.experimental.pallas import tpu as pltpu

F32 = jnp.float32
BF16 = jnp.bfloat16

ATTN_HEAD_DIM = 64
DILATED_PATTERNS = ((128, 1), (512, 4), (2048, 16))
MLSTM_HEADS = 4
MLSTM_CONV = 4
MLSTM_CHUNK = 128
PEER_HEADS = 8
PEER_N_KEYS = 128
PEER_TOPK = 16
DEPTH = 1
DEEPNORM_ALPHA = (2.0 * DEPTH) ** 0.25
LN_EPS = 1e-5

LANES = 128
NEG_BIG = -1e30
VMEM_LIMIT = 56 * 1024 * 1024


def _cparams(*sem):
    return pltpu.CompilerParams(dimension_semantics=sem, vmem_limit_bytes=VMEM_LIMIT)


def _in_proj_kernel(x_ref, w_ref, wg_ref, *out_refs):
    xb = x_ref[...].astype(BF16)
    g_ref = out_refs[-1]
    for i, o_ref in enumerate(out_refs[:-1]):
        n = o_ref.shape[1]
        o_ref[...] = jnp.dot(xb, w_ref[:, i * n:(i + 1) * n], preferred_element_type=F32).astype(o_ref.dtype)
    g_ref[...] = jnp.dot(xb, wg_ref[...], preferred_element_type=F32)


def _in_proj(x2, w_main, w_gate, tm):
    T, D = x2.shape
    n_main = w_main.shape[1]
    n_out = 6
    wcol = n_main // n_out
    out_shape = [jax.ShapeDtypeStruct((T, wcol), BF16) for _ in range(n_out)]
    out_shape.append(jax.ShapeDtypeStruct((T, LANES), F32))
    out_specs = [pl.BlockSpec((tm, wcol), lambda i: (i, 0)) for _ in range(n_out)]
    out_specs.append(pl.BlockSpec((tm, LANES), lambda i: (i, 0)))
    return pl.pallas_call(
        _in_proj_kernel,
        grid=(T // tm,),
        in_specs=[pl.BlockSpec((tm, D), lambda i: (i, 0)),
                  pl.BlockSpec((D, n_main), lambda i: (0, 0)),
                  pl.BlockSpec((D, LANES), lambda i: (0, 0))],
        out_specs=out_specs,
        out_shape=out_shape,
        compiler_params=_cparams("parallel"),
        name="in_proj",
    )(x2, w_main, w_gate)


def _attn_bias_table(S, blk):
    nrel = S // blk
    rel = np.arange(nrel)[:, None, None] * blk
    delta = rel + np.arange(blk)[None, None, :] - np.arange(blk)[None, :, None]
    cnt = np.zeros(delta.shape, np.float64)
    for window, dil in DILATED_PATTERNS:
        cnt += (delta >= 0) & (delta <= window) & (delta % dil == 0)
    bias = np.where(cnt > 0, np.log(np.maximum(cnt, 1.0)), NEG_BIG)
    return jnp.asarray(bias, F32)


def _attn_kernel(q_ref, k_ref, vt_ref, bias_ref, o_ref, *, blk, scale):
    S = q_ref.shape[0]
    nblk = S // blk
    lo = lax.broadcasted_iota(jnp.int32, (blk, LANES), 1) < ATTN_HEAD_DIM
    row_lo = lax.broadcasted_iota(jnp.int32, (LANES, blk), 0) < ATTN_HEAD_DIM

    def q_block(i, carry):
        r0 = pl.multiple_of(i * blk, blk)
        q = q_ref[pl.ds(r0, blk), :]
        qs = (q.astype(F32) * scale).astype(BF16)
        zero = jnp.zeros_like(qs)
        q2 = jnp.concatenate([jnp.where(lo, qs, zero), jnp.where(lo, zero, qs)], axis=0)

        def k_block(j, st):
            m, l, acc = st
            c0 = pl.multiple_of(j * blk, blk)
            kt = k_ref[pl.ds(c0, blk), :]
            s = lax.dot_general(kt, q2, (((1,), (1,)), ((), ())), preferred_element_type=F32)
            b = bias_ref[i - j]
            s = s + jnp.concatenate([b, b], axis=1)
            m_new = jnp.maximum(m, jnp.max(s, axis=0, keepdims=True))
            alpha = jnp.exp(m - m_new)
            p = jnp.exp(s - m_new)
            l = alpha * l + jnp.sum(p, axis=0, keepdims=True)
            acc = alpha * acc + jnp.dot(vt_ref[j], p.astype(BF16), preferred_element_type=F32)
            return m_new, l, acc

        init = (jnp.full((1, 2 * blk), -jnp.inf, F32), jnp.zeros((1, 2 * blk), F32),
                jnp.zeros((LANES, 2 * blk), F32))
        m, l, acc = lax.fori_loop(0, i + 1, k_block, init)
        out = acc / l
        out_t = jnp.where(row_lo, out[:, :blk], out[:, blk:])
        o_ref[pl.ds(r0, blk), :] = out_t.T.astype(o_ref.dtype)
        return carry

    lax.fori_loop(0, nblk, q_block, 0)


def _dilated_attention(q, k, v, B, S):
    T, W = q.shape
    blk = min(256, S)
    nblk = S // blk
    bias = _attn_bias_table(S, blk)
    npair = W // LANES
    vt = v.reshape(B, nblk, blk, npair, LANES).transpose(0, 3, 1, 4, 2)
    spec = pl.BlockSpec((S, LANES), lambda b, h: (b, h))
    return pl.pallas_call(
        functools.partial(_attn_kernel, blk=blk, scale=ATTN_HEAD_DIM ** -0.5),
        grid=(B, npair),
        in_specs=[spec, spec,
                  pl.BlockSpec((None, None, nblk, LANES, blk), lambda b, h: (b, h, 0, 0, 0)),
                  pl.BlockSpec(bias.shape, lambda b, h: (0, 0, 0))],
        out_specs=spec,
        out_shape=jax.ShapeDtypeStruct((T, W), BF16),
        compiler_params=_cparams("parallel", "parallel"),
        name="dilated_attn",
    )(q, k, vt, bias)


def _log_sigmoid(x):
    return jnp.minimum(x, 0.0) - jnp.log1p(jnp.exp(-jnp.abs(x)))


def _mlstm_kernel(xm_ref, vm_ref, om_ref, gcol_ref, grow_ref, gb_row_ref, gb_col_ref, cw_ref, cb_ref,
                  wq_ref, wk_ref, ng_ref, o_ref, q_s, k_s, c_s):
    S, W = xm_ref.shape
    H = MLSTM_HEADS
    E = W // H
    CH = MLSTM_CHUNK
    NC = S // CH
    hi = lax.Precision.HIGHEST

    x32 = xm_ref[...].astype(F32)
    row = lax.broadcasted_iota(jnp.int32, (S, W), 0)
    xc = x32 * cw_ref[MLSTM_CONV - 1:MLSTM_CONV, :] + cb_ref[...]
    for j in range(1, MLSTM_CONV):
        sh = jnp.where(row >= j, pltpu.roll(x32, j, axis=0), 0.0)
        xc = xc + sh * cw_ref[MLSTM_CONV - 1 - j:MLSTM_CONV - j, :]
    xc = (xc * jax.nn.sigmoid(xc)).astype(BF16)
    for h in range(H):
        xh = xc[:, h * E:(h + 1) * E]
        q_s[h] = (jnp.dot(xh, wq_ref[h], preferred_element_type=F32) * (E ** -0.5)).astype(BF16)
        k_s[h] = jnp.dot(xh, wk_ref[h], preferred_element_type=F32).astype(BF16)

    c_s[...] = jnp.zeros_like(c_s)
    ri = lax.broadcasted_iota(jnp.int32, (CH, CH), 0)
    ci = lax.broadcasted_iota(jnp.int32, (CH, CH), 1)
    tril = ri >= ci
    tril_f = tril.astype(F32)
    triu_f = (ri <= ci).astype(F32)

    def chunk(c, carry):
        r0 = pl.multiple_of(c * CH, CH)
        gc = gcol_ref[c] + gb_row_ref[...]
        gr = grow_ref[c] + gb_col_ref[...]
        bc_col = jnp.dot(tril_f, _log_sigmoid(gc), precision=hi, preferred_element_type=F32)
        bc_row = jnp.dot(_log_sigmoid(gr), triu_f, precision=hi, preferred_element_type=F32)
        new_carry = []
        for h in range(H):
            n_prev, m_prev = carry[h]
            bcol = bc_col[:, H + h:H + h + 1]
            brow = bc_row[H + h:H + h + 1, :]
            icol = gc[:, h:h + 1]
            irow = gr[h:h + 1, :]
            qc = q_s[h, pl.ds(r0, CH), :]
            kc = k_s[h, pl.ds(r0, CH), :]
            vc = vm_ref[pl.ds(r0, CH), h * E:(h + 1) * E]
            C = c_s[h]

            dlog = jnp.where(tril, bcol - brow + irow, -jnp.inf)
            m_inter = bcol + m_prev
            m_t = jnp.maximum(m_inter, jnp.max(dlog, axis=1, keepdims=True))
            w_inter = jnp.exp(m_inter - m_t)
            qk = lax.dot_general(qc, kc, (((1,), (1,)), ((), ())), preferred_element_type=F32)
            P = jnp.exp(dlog - m_t) * qk
            inter = lax.dot_general(qc, C.astype(BF16), (((1,), (1,)), ((), ())), preferred_element_type=F32)
            num = w_inter * inter + jnp.dot(P.astype(BF16), vc, preferred_element_type=F32)
            qn = jnp.sum(qc.astype(F32) * n_prev, axis=1, keepdims=True)
            den = w_inter * qn + jnp.sum(P, axis=1, keepdims=True)
            ht = num / jnp.maximum(jnp.abs(den), jnp.exp(-m_t))

            bL = bcol[CH - 1:CH, :]
            g_row = bL - brow + irow
            g_col = bL - bcol + icol
            m_new = jnp.maximum(bL + m_prev, jnp.max(g_row, axis=1, keepdims=True))
            a = jnp.exp(bL + m_prev - m_new)
            wk_col = jnp.exp(g_col - m_new)
            vw = (vc.astype(F32) * wk_col).astype(BF16)
            c_s[h] = a * C + lax.dot_general(vw, kc, (((0,), (0,)), ((), ())), preferred_element_type=F32)
            n_new = a * n_prev + jnp.sum(kc.astype(F32) * wk_col, axis=0, keepdims=True)
            new_carry.append((n_new, m_new))

            mu = jnp.mean(ht, axis=1, keepdims=True)
            var = jnp.mean(jnp.square(ht - mu), axis=1, keepdims=True)
            hn = (ht - mu) * lax.rsqrt(var + LN_EPS) * ng_ref[:, h * E:(h + 1) * E]
            og = om_ref[pl.ds(r0, CH), h * E:(h + 1) * E].astype(F32)
            o_ref[pl.ds(r0, CH), h * E:(h + 1) * E] = (jax.nn.sigmoid(og) * hn).astype(o_ref.dtype)
        return tuple(new_carry)

    init = tuple((jnp.zeros((1, E), F32), jnp.zeros((1, 1), F32)) for _ in range(H))
    lax.fori_loop(0, NC, chunk, init)


def _mlstm(xm, vm, om, gates, conv_w, conv_b, w_mq, w_mk, b_igate, b_fgate, mh_norm_g, B, S):
    T, W = xm.shape
    H = MLSTM_HEADS
    E = W // H
    CH = MLSTM_CHUNK
    NC = S // CH
    gcol = gates.reshape(B, NC, CH, 2 * H)
    grow = gcol.transpose(0, 1, 3, 2)
    gb = jnp.concatenate([b_igate, b_fgate]).astype(F32)
    seq = pl.BlockSpec((S, W), lambda b: (b, 0))
    full2 = lambda shape: pl.BlockSpec(shape, lambda b: (0,) * len(shape))
    return pl.pallas_call(
        _mlstm_kernel,
        grid=(B,),
        in_specs=[seq, seq, seq,
                  pl.BlockSpec((None, NC, CH, 2 * H), lambda b: (b, 0, 0, 0)),
                  pl.BlockSpec((None, NC, 2 * H, CH), lambda b: (b, 0, 0, 0)),
                  full2((1, 2 * H)), full2((2 * H, 1)), full2((MLSTM_CONV, W)), full2((1, W)),
                  full2((H, E, E)), full2((H, E, E)), full2((1, W))],
        out_specs=seq,
        out_shape=jax.ShapeDtypeStruct((T, W), BF16),
        scratch_shapes=[pltpu.VMEM((H, S, E), BF16), pltpu.VMEM((H, S, E), BF16), pltpu.VMEM((H, E, E), F32)],
        compiler_params=_cparams("parallel"),
        name="mlstm",
    )(xm, vm, om, gcol, grow, gb[None, :], gb[:, None], conv_w.astype(F32), conv_b[None, :].astype(F32),
      w_mq.astype(BF16), w_mk.astype(BF16), mh_norm_g[None, :].astype(F32))


def _layer_norm_rows(y, g, b):
    mu = jnp.mean(y, axis=1, keepdims=True)
    d = y - mu
    var = jnp.mean(d * d, axis=1, keepdims=True)
    return d * lax.rsqrt(var + LN_EPS) * g + b


def _out_proj_kernel(x_ref, a_ref, m_ref, wa_ref, wm_ref, g_ref, b_ref, o_ref):
    mix = jnp.dot(a_ref[...], wa_ref[...], preferred_element_type=F32)
    mix = mix + jnp.dot(m_ref[...], wm_ref[...], preferred_element_type=F32)
    o_ref[...] = _layer_norm_rows(DEEPNORM_ALPHA * x_ref[...] + mix, g_ref[...], b_ref[...])


def _out_proj_ln(x2, attn, mlstm, w_out, g, b, tm):
    T, D = x2.shape
    Wa = attn.shape[1]
    Wm = mlstm.shape[1]
    wa = w_out[:Wa].astype(BF16)
    wm = w_out[Wa:].astype(BF16)
    return pl.pallas_call(
        _out_proj_kernel,
        grid=(T // tm,),
        in_specs=[pl.BlockSpec((tm, D), lambda i: (i, 0)),
                  pl.BlockSpec((tm, Wa), lambda i: (i, 0)),
                  pl.BlockSpec((tm, Wm), lambda i: (i, 0)),
                  pl.BlockSpec((Wa, D), lambda i: (0, 0)),
                  pl.BlockSpec((Wm, D), lambda i: (0, 0)),
                  pl.BlockSpec((1, D), lambda i: (0, 0)),
                  pl.BlockSpec((1, D), lambda i: (0, 0))],
        out_specs=pl.BlockSpec((tm, D), lambda i: (i, 0)),
        out_shape=jax.ShapeDtypeStruct((T, D), F32),
        compiler_params=_cparams("parallel"),
        name="out_proj_ln",
    )(x2, attn, mlstm, wa, wm, g[None, :].astype(F32), b[None, :].astype(F32))


def _topk_rows(s, k, payload=None):
    R, N = s.shape
    rid = lax.broadcasted_iota(jnp.int32, (R, N), 0)
    vals, idxs = [], []
    for _ in range(k):
        mx = jnp.max(s, axis=0, keepdims=True)
        ix = jnp.min(jnp.where(s == mx, rid, R), axis=0, keepdims=True)
        sel = rid == ix
        vals.append(mx)
        if payload is None:
            idxs.append(ix)
        else:
            idxs.append(jnp.sum(jnp.where(sel, payload, 0), axis=0, keepdims=True))
        s = jnp.where(sel, -jnp.inf, s)
    return jnp.concatenate(vals, axis=0), jnp.concatenate(idxs, axis=0)


def _peer_route_kernel(h_ref, wq_ref, keys_ref, eidx_ref, gate_ref, *, sub):
    tm = h_ref.shape[0]
    K = PEER_TOPK
    hb = h_ref[...].astype(BF16)
    qry = jnp.dot(hb, wq_ref[...], preferred_element_type=F32).astype(BF16)
    pairs = [(a, b) for a in range(K) for b in range(K) if (a + 1) * (b + 1) <= K]
    for t0 in range(0, tm, sub):
        e_rows, g_rows = [], []
        for h in range(PEER_HEADS):
            tops = []
            for p in range(2):
                c0 = (h * 2 + p) * LANES
                qhp = qry[t0:t0 + sub, c0:c0 + LANES]
                sc = lax.dot_general(keys_ref[h, p], qhp, (((1,), (1,)), ((), ())),
                                     preferred_element_type=F32)
                tops.append(_topk_rows(sc, K))
            (v1, i1), (v2, i2) = tops
            cand = jnp.concatenate([v1[a:a + 1] + v2[b:b + 1] for a, b in pairs], axis=0)
            cidx = jnp.concatenate([i1[a:a + 1] * PEER_N_KEYS + i2[b:b + 1] for a, b in pairs], axis=0)
            best, eidx = _topk_rows(cand, K, payload=cidx)
            ex = jnp.exp(best - best[0:1])
            g_rows.append(ex / jnp.sum(ex, axis=0, keepdims=True))
            e_rows.append(eidx)
        eidx_ref[:, t0:t0 + sub] = jnp.concatenate(e_rows, axis=0)
        gate_ref[:, t0:t0 + sub] = jnp.concatenate(g_rows, axis=0)


def _peer_route(h1, w_query, sub_keys, tm):
    T, D = h1.shape
    NQ = w_query.shape[1]
    R = PEER_HEADS * PEER_TOPK
    return pl.pallas_call(
        functools.partial(_peer_route_kernel, sub=LANES),
        grid=(T // tm,),
        in_specs=[pl.BlockSpec((tm, D), lambda i: (i, 0)),
                  pl.BlockSpec((D, NQ), lambda i: (0, 0)),
                  pl.BlockSpec(sub_keys.shape, lambda i: (0, 0, 0, 0))],
        out_specs=[pl.BlockSpec((R, tm), lambda i: (0, i)), pl.BlockSpec((R, tm), lambda i: (0, i))],
        out_shape=[jax.ShapeDtypeStruct((R, T), jnp.int32), jax.ShapeDtypeStruct((R, T), F32)],
        compiler_params=_cparams("parallel"),
        name="peer_route",
    )(h1, w_query.astype(BF16), sub_keys.astype(BF16))


PEER_TOKENS_PER_STEP = 32
PEER_SLOTS = 8
LANE_TILES = 16


def _gelu_exact(x):
    return 0.5 * x * (1.0 + lax.erf(x * (2.0 ** -0.5)))


def _peer_expert_kernel(idx_ref, h_ref, gate_ref, g_ref, b_ref, table_ref, o_ref, buf, sem, acc, *, nsteps):
    tb, D = h_ref.shape
    R = buf.shape[2]
    half = LANE_TILES // 2
    ahead = PEER_SLOTS - 1
    i = pl.program_id(0)

    def start_row(row, slot, k):
        pltpu.make_async_copy(table_ref.at[idx_ref[0, row, k]], buf.at[slot, :, k, :], sem.at[slot]).start(priority=k % 2)

    def wait(slot):
        pltpu.make_async_copy(buf.at[slot], buf.at[slot], sem.at[slot]).wait()

    @pl.when(i == 0)
    def _():
        for s in range(ahead):
            for k in range(R):
                start_row(s, s, k)

    lane_tok = lax.broadcasted_iota(jnp.int32, (1, tb), 1)
    SUB = 8
    npiece = R // SUB
    per_piece = 3
    mid = (R - 2 * npiece * per_piece) // 2

    def token(j, s):
        islot = (s + ahead) % PEER_SLOTS
        pending = iter(range(R))

        def start_rows(n):
            for _ in range(n):
                start_row(j + ahead, islot, next(pending))

        x = h_ref[pl.ds(j, 1), :]
        xb = [jnp.broadcast_to(x[:, c * LANES:(c + 1) * LANES], (SUB, LANES)) for c in range(half)]
        parts = []
        for p in range(npiece):
            rows = slice(p * SUB, (p + 1) * SUB)
            t = buf[s, 0, rows, :] * xb[0]
            for c in range(1, half):
                t = t + buf[s, c, rows, :] * xb[c]
            parts.append(t)
            start_rows(per_piece)
        a = jnp.sum(jnp.concatenate(parts, axis=0), axis=1, keepdims=True)
        gcol = jnp.sum(jnp.where(lane_tok == j, gate_ref[...], 0.0), axis=1, keepdims=True)
        start_rows(mid)
        coef = jnp.broadcast_to(_gelu_exact(a) * gcol, (R, LANES))
        start_rows(R - 2 * npiece * per_piece - mid)
        outs = [None] * half
        for p in range(npiece):
            rows = slice(p * SUB, (p + 1) * SUB)
            cb = coef[rows, :]
            for c in range(half):
                t = buf[s, half + c, rows, :] * cb
                outs[c] = t if outs[c] is None else outs[c] + t
            start_rows(per_piece)
        acc[pl.ds(j, 1), :] = jnp.concatenate([jnp.sum(o, axis=0, keepdims=True) for o in outs], axis=1)

    def group(jj, carry):
        for s in range(PEER_SLOTS):
            wait(s)
            token(jj * PEER_SLOTS + s, s)
        return carry

    lax.fori_loop(0, tb // PEER_SLOTS, group, 0)

    @pl.when(i == nsteps - 1)
    def _():
        for s in range(ahead):
            wait(s)

    o_ref[...] = _layer_norm_rows(DEEPNORM_ALPHA * h_ref[...] + acc[...], g_ref[...], b_ref[...])


def _peer_experts(h1, eidx_t, gate_t, w_down, w_up, g, b):
    T, D = h1.shape
    R = eidx_t.shape[0]
    tb = PEER_TOKENS_PER_STEP
    nsteps = T // tb
    ahead = PEER_SLOTS - 1
    assert tb % PEER_SLOTS == 0 and 2 * D == LANE_TILES * LANES
    table = jnp.concatenate([w_down, w_up], axis=1).reshape(-1, LANE_TILES, LANES)
    eidx = jnp.pad(eidx_t.T, ((0, tb), (0, 0)))
    win = (jnp.arange(nsteps)[:, None] * tb + jnp.arange(tb + ahead)[None, :])
    eidx_win = eidx[win]
    gate_blk = gate_t.reshape(R, nsteps, tb).transpose(1, 0, 2)
    return pl.pallas_call(
        functools.partial(_peer_expert_kernel, nsteps=nsteps),
        grid=(nsteps,),
        in_specs=[pl.BlockSpec((1, tb + ahead, R), lambda i: (i, 0, 0), memory_space=pltpu.SMEM),
                  pl.BlockSpec((tb, D), lambda i: (i, 0)),
                  pl.BlockSpec((None, R, tb), lambda i: (i, 0, 0)),
                  pl.BlockSpec((1, D), lambda i: (0, 0)),
                  pl.BlockSpec((1, D), lambda i: (0, 0)),
                  pl.BlockSpec(memory_space=pl.ANY)],
        out_specs=pl.BlockSpec((tb, D), lambda i: (i, 0)),
        out_shape=jax.ShapeDtypeStruct((T, D), F32),
        scratch_shapes=[pltpu.VMEM((PEER_SLOTS, LANE_TILES, R, LANES), F32),
                        pltpu.SemaphoreType.DMA((PEER_SLOTS,)),
                        pltpu.VMEM((tb, D), F32)],
        compiler_params=_cparams("arbitrary"),
        name="peer_experts",
    )(eidx_win, h1, gate_blk, g[None, :].astype(F32), b[None, :].astype(F32), table)


def kernel(x, w_in, conv_w, conv_b, w_mq, w_mk, b_igate, b_fgate, mh_norm_g, w_out, ln1_g, ln1_b,
           peer_w_query, peer_sub_keys, peer_w_down, peer_w_up, ln2_g, ln2_b):
    B, S, D = x.shape
    T = B * S
    H = MLSTM_HEADS
    h = x.reshape(T, D)
    for l in range(DEPTH):
        n_gate = 2 * H
        w_main = w_in[l][:, :-n_gate].astype(BF16)
        w_gate = jnp.pad(w_in[l][:, -n_gate:], ((0, 0), (0, LANES - n_gate))).astype(BF16)
        q_a, k_a, v_a, x_m, v_m, o_m, gates = _in_proj(h, w_main, w_gate, tm=min(512, T))
        attn = _dilated_attention(q_a, k_a, v_a, B, S)
        mlstm = _mlstm(x_m, v_m, o_m, gates[:, :n_gate], conv_w[l], conv_b[l], w_mq[l], w_mk[l],
                       b_igate[l], b_fgate[l], mh_norm_g[l], B, S)
        h1 = _out_proj_ln(h, attn, mlstm, w_out[l], ln1_g[l], ln1_b[l], tm=min(512, T))
        eidx_t, gate_t = _peer_route(h1, peer_w_query[l], peer_sub_keys[l], tm=min(512, T))
        h = _peer_experts(h1, eidx_t, gate_t, peer_w_down[l], peer_w_up[l], ln2_g[l], ln2_b[l])
    return h.reshape(B, S, D)
```

```python
import functools
import math

import numpy as np
import jax
import jax.numpy as jnp
from jax import lax
from jax.experimental import pallas as pl
from jax.experimental.pallas import tpu as pltpu

F32 = jnp.float32
BF16 = jnp.bfloat16

ATTN_HEAD_DIM = 64
DILATED_PATTERNS = ((128, 1), (512, 4), (2048, 16))
MLSTM_HEADS = 4
MLSTM_CONV = 4
MLSTM_CHUNK = 128
PEER_HEADS = 8
PEER_N_KEYS = 128
PEER_TOPK = 16
DEPTH = 1
DEEPNORM_ALPHA = (2.0 * DEPTH) ** 0.25
LN_EPS = 1e-5

LANES = 128
NEG_BIG = -1e30
VMEM_LIMIT = 56 * 1024 * 1024


def _cparams(*sem):
    return pltpu.CompilerParams(dimension_semantics=sem, vmem_limit_bytes=VMEM_LIMIT)


def _in_proj_kernel(x_ref, w_ref, wg_ref, *out_refs):
    xb = x_ref[...].astype(BF16)
    g_ref = out_refs[-1]
    for i, o_ref in enumerate(out_refs[:-1]):
        n = o_ref.shape[1]
        o_ref[...] = jnp.dot(xb, w_ref[:, i * n:(i + 1) * n], preferred_element_type=F32).astype(o_ref.dtype)
    g_ref[...] = jnp.dot(xb, wg_ref[...], preferred_element_type=F32)


def _in_proj(x2, w_main, w_gate, tm):
    T, D = x2.shape
    n_main = w_main.shape[1]
    n_out = 6
    wcol = n_main // n_out
    out_shape = [jax.ShapeDtypeStruct((T, wcol), BF16) for _ in range(n_out)]
    out_shape.append(jax.ShapeDtypeStruct((T, LANES), F32))
    out_specs = [pl.BlockSpec((tm, wcol), lambda i: (i, 0)) for _ in range(n_out)]
    out_specs.append(pl.BlockSpec((tm, LANES), lambda i: (i, 0)))
    return pl.pallas_call(
        _in_proj_kernel,
        grid=(T // tm,),
        in_specs=[pl.BlockSpec((tm, D), lambda i: (i, 0)),
                  pl.BlockSpec((D, n_main), lambda i: (0, 0)),
                  pl.BlockSpec((D, LANES), lambda i: (0, 0))],
        out_specs=out_specs,
        out_shape=out_shape,
        compiler_params=_cparams("parallel"),
        name="in_proj",
    )(x2, w_main, w_gate)


def _attn_bias_table(S, blk):
    nrel = S // blk
    rel = np.arange(nrel)[:, None, None] * blk
    delta = rel + np.arange(blk)[None, None, :] - np.arange(blk)[None, :, None]
    cnt = np.zeros(delta.shape, np.float64)
    for window, dil in DILATED_PATTERNS:
        cnt += (delta >= 0) & (delta <= window) & (delta % dil == 0)
    bias = np.where(cnt > 0, np.log(np.maximum(cnt, 1.0)), NEG_BIG)
    return jnp.asarray(bias, F32)


def _attn_kernel(q_ref, k_ref, vt_ref, bias_ref, o_ref, *, blk, scale):
    S = q_ref.shape[0]
    nblk = S // blk
    lo = lax.broadcasted_iota(jnp.int32, (blk, LANES), 1) < ATTN_HEAD_DIM
    row_lo = lax.broadcasted_iota(jnp.int32, (LANES, blk), 0) < ATTN_HEAD_DIM

    def q_block(i, carry):
        r0 = pl.multiple_of(i * blk, blk)
        q = q_ref[pl.ds(r0, blk), :]
        qs = (q.astype(F32) * scale).astype(BF16)
        zero = jnp.zeros_like(qs)
        q2 = jnp.concatenate([jnp.where(lo, qs, zero), jnp.where(lo, zero, qs)], axis=0)

        def k_block(j, st):
            m, l, acc = st
            c0 = pl.multiple_of(j * blk, blk)
            kt = k_ref[pl.ds(c0, blk), :]
            s = lax.dot_general(kt, q2, (((1,), (1,)), ((), ())), preferred_element_type=F32)
            b = bias_ref[i - j]
            s = s + jnp.concatenate([b, b], axis=1)
            m_new = jnp.maximum(m, jnp.max(s, axis=0, keepdims=True))
            alpha = jnp.exp(m - m_new)
            p = jnp.exp(s - m_new)
            l = alpha * l + jnp.sum(p, axis=0, keepdims=True)
            acc = alpha * acc + jnp.dot(vt_ref[j], p.astype(BF16), preferred_element_type=F32)
            return m_new, l, acc

        init = (jnp.full((1, 2 * blk), -jnp.inf, F32), jnp.zeros((1, 2 * blk), F32),
                jnp.zeros((LANES, 2 * blk), F32))
        m, l, acc = lax.fori_loop(0, i + 1, k_block, init)
        out = acc / l
        out_t = jnp.where(row_lo, out[:, :blk], out[:, blk:])
        o_ref[pl.ds(r0, blk), :] = out_t.T.astype(o_ref.dtype)
        return carry

    lax.fori_loop(0, nblk, q_block, 0)


def _dilated_attention(q, k, v, B, S):
    T, W = q.shape
    blk = min(256, S)
    nblk = S // blk
    bias = _attn_bias_table(S, blk)
    npair = W // LANES
    vt = v.reshape(B, nblk, blk, npair, LANES).transpose(0, 3, 1, 4, 2)
    spec = pl.BlockSpec((S, LANES), lambda b, h: (b, h))
    return pl.pallas_call(
        functools.partial(_attn_kernel, blk=blk, scale=ATTN_HEAD_DIM ** -0.5),
        grid=(B, npair),
        in_specs=[spec, spec,
                  pl.BlockSpec((None, None, nblk, LANES, blk), lambda b, h: (b, h, 0, 0, 0)),
                  pl.BlockSpec(bias.shape, lambda b, h: (0, 0, 0))],
        out_specs=spec,
        out_shape=jax.ShapeDtypeStruct((T, W), BF16),
        compiler_params=_cparams("parallel", "parallel"),
        name="dilated_attn",
    )(q, k, vt, bias)


def _log_sigmoid(x):
    return jnp.minimum(x, 0.0) - jnp.log1p(jnp.exp(-jnp.abs(x)))


def _mlstm_kernel(xm_ref, vm_ref, om_ref, gcol_ref, grow_ref, gb_row_ref, gb_col_ref, cw_ref, cb_ref,
                  wq_ref, wk_ref, ng_ref, o_ref, q_s, k_s, c_s):
    S, W = xm_ref.shape
    H = MLSTM_HEADS
    E = W // H
    CH = MLSTM_CHUNK
    NC = S // CH
    hi = lax.Precision.HIGHEST

    x32 = xm_ref[...].astype(F32)
    row = lax.broadcasted_iota(jnp.int32, (S, W), 0)
    xc = x32 * cw_ref[MLSTM_CONV - 1:MLSTM_CONV, :] + cb_ref[...]
    for j in range(1, MLSTM_CONV):
        sh = jnp.where(row >= j, pltpu.roll(x32, j, axis=0), 0.0)
        xc = xc + sh * cw_ref[MLSTM_CONV - 1 - j:MLSTM_CONV - j, :]
    xc = (xc * jax.nn.sigmoid(xc)).astype(BF16)
    for h in range(H):
        xh = xc[:, h * E:(h + 1) * E]
        q_s[h] = (jnp.dot(xh, wq_ref[h], preferred_element_type=F32) * (E ** -0.5)).astype(BF16)
        k_s[h] = jnp.dot(xh, wk_ref[h], preferred_element_type=F32).astype(BF16)

    c_s[...] = jnp.zeros_like(c_s)
    ri = lax.broadcasted_iota(jnp.int32, (CH, CH), 0)
    ci = lax.broadcasted_iota(jnp.int32, (CH, CH), 1)
    tril = ri >= ci
    tril_f = tril.astype(F32)
    triu_f = (ri <= ci).astype(F32)

    def chunk(c, carry):
        r0 = pl.multiple_of(c * CH, CH)
        gc = gcol_ref[c] + gb_row_ref[...]
        gr = grow_ref[c] + gb_col_ref[...]
        bc_col = jnp.dot(tril_f, _log_sigmoid(gc), precision=hi, preferred_element_type=F32)
        bc_row = jnp.dot(_log_sigmoid(gr), triu_f, precision=hi, preferred_element_type=F32)
        new_carry = []
        for h in range(H):
            n_prev, m_prev = carry[h]
            bcol = bc_col[:, H + h:H + h + 1]
            brow = bc_row[H + h:H + h + 1, :]
            icol = gc[:, h:h + 1]
            irow = gr[h:h + 1, :]
            qc = q_s[h, pl.ds(r0, CH), :]
            kc = k_s[h, pl.ds(r0, CH), :]
            vc = vm_ref[pl.ds(r0, CH), h * E:(h + 1) * E]
            C = c_s[h]

            dlog = jnp.where(tril, bcol - brow + irow, -jnp.inf)
            m_inter = bcol + m_prev
            m_t = jnp.maximum(m_inter, jnp.max(dlog, axis=1, keepdims=True))
            w_inter = jnp.exp(m_inter - m_t)
            qk = lax.dot_general(qc, kc, (((1,), (1,)), ((), ())), preferred_element_type=F32)
            P = jnp.exp(dlog - m_t) * qk
            inter = lax.dot_general(qc, C.astype(BF16), (((1,), (1,)), ((), ())), preferred_element_type=F32)
            num = w_inter * inter + jnp.dot(P.astype(BF16), vc, preferred_element_type=F32)
            qn = jnp.sum(qc.astype(F32) * n_prev, axis=1, keepdims=True)
            den = w_inter * qn + jnp.sum(P, axis=1, keepdims=True)
            ht = num / jnp.maximum(jnp.abs(den), jnp.exp(-m_t))

            bL = bcol[CH - 1:CH, :]
            g_row = bL - brow + irow
            g_col = bL - bcol + icol
            m_new = jnp.maximum(bL + m_prev, jnp.max(g_row, axis=1, keepdims=True))
            a = jnp.exp(bL + m_prev - m_new)
            wk_col = jnp.exp(g_col - m_new)
            vw = (vc.astype(F32) * wk_col).astype(BF16)
            c_s[h] = a * C + lax.dot_general(vw, kc, (((0,), (0,)), ((), ())), preferred_element_type=F32)
            n_new = a * n_prev + jnp.sum(kc.astype(F32) * wk_col, axis=0, keepdims=True)
            new_carry.append((n_new, m_new))

            mu = jnp.mean(ht, axis=1, keepdims=True)
            var = jnp.mean(jnp.square(ht - mu), axis=1, keepdims=True)
            hn = (ht - mu) * lax.rsqrt(var + LN_EPS) * ng_ref[:, h * E:(h + 1) * E]
            og = om_ref[pl.ds(r0, CH), h * E:(h + 1) * E].astype(F32)
            o_ref[pl.ds(r0, CH), h * E:(h + 1) * E] = (jax.nn.sigmoid(og) * hn).astype(o_ref.dtype)
        return tuple(new_carry)

    init = tuple((jnp.zeros((1, E), F32), jnp.zeros((1, 1), F32)) for _ in range(H))
    lax.fori_loop(0, NC, chunk, init)


def _mlstm(xm, vm, om, gates, conv_w, conv_b, w_mq, w_mk, b_igate, b_fgate, mh_norm_g, B, S):
    T, W = xm.shape
    H = MLSTM_HEADS
    E = W // H
    CH = MLSTM_CHUNK
    NC = S // CH
    gcol = gates.reshape(B, NC, CH, 2 * H)
    grow = gcol.transpose(0, 1, 3, 2)
    gb = jnp.concatenate([b_igate, b_fgate]).astype(F32)
    seq = pl.BlockSpec((S, W), lambda b: (b, 0))
    full2 = lambda shape: pl.BlockSpec(shape, lambda b: (0,) * len(shape))
    return pl.pallas_call(
        _mlstm_kernel,
        grid=(B,),
        in_specs=[seq, seq, seq,
                  pl.BlockSpec((None, NC, CH, 2 * H), lambda b: (b, 0, 0, 0)),
                  pl.BlockSpec((None, NC, 2 * H, CH), lambda b: (b, 0, 0, 0)),
                  full2((1, 2 * H)), full2((2 * H, 1)), full2((MLSTM_CONV, W)), full2((1, W)),
                  full2((H, E, E)), full2((H, E, E)), full2((1, W))],
        out_specs=seq,
        out_shape=jax.ShapeDtypeStruct((T, W), BF16),
        scratch_shapes=[pltpu.VMEM((H, S, E), BF16), pltpu.VMEM((H, S, E), BF16), pltpu.VMEM((H, E, E), F32)],
        compiler_params=_cparams("parallel"),
        name="mlstm",
    )(xm, vm, om, gcol, grow, gb[None, :], gb[:, None], conv_w.astype(F32), conv_b[None, :].astype(F32),
      w_mq.astype(BF16), w_mk.astype(BF16), mh_norm_g[None, :].astype(F32))


def _layer_norm_rows(y, g, b):
    mu = jnp.mean(y, axis=1, keepdims=True)
    d = y - mu
    var = jnp.mean(d * d, axis=1, keepdims=True)
    return d * lax.rsqrt(var + LN_EPS) * g + b


def _out_proj_kernel(x_ref, a_ref, m_ref, wa_ref, wm_ref, g_ref, b_ref, o_ref):
    mix = jnp.dot(a_ref[...], wa_ref[...], preferred_element_type=F32)
    mix = mix + jnp.dot(m_ref[...], wm_ref[...], preferred_element_type=F32)
    o_ref[...] = _layer_norm_rows(DEEPNORM_ALPHA * x_ref[...] + mix, g_ref[...], b_ref[...])


def _out_proj_ln(x2, attn, mlstm, w_out, g, b, tm):
    T, D = x2.shape
    Wa = attn.shape[1]
    Wm = mlstm.shape[1]
    wa = w_out[:Wa].astype(BF16)
    wm = w_out[Wa:].astype(BF16)
    return pl.pallas_call(
        _out_proj_kernel,
        grid=(T // tm,),
        in_specs=[pl.BlockSpec((tm, D), lambda i: (i, 0)),
                  pl.BlockSpec((tm, Wa), lambda i: (i, 0)),
                  pl.BlockSpec((tm, Wm), lambda i: (i, 0)),
                  pl.BlockSpec((Wa, D), lambda i: (0, 0)),
                  pl.BlockSpec((Wm, D), lambda i: (0, 0)),
                  pl.BlockSpec((1, D), lambda i: (0, 0)),
                  pl.BlockSpec((1, D), lambda i: (0, 0))],
        out_specs=pl.BlockSpec((tm, D), lambda i: (i, 0)),
        out_shape=jax.ShapeDtypeStruct((T, D), F32),
        compiler_params=_cparams("parallel"),
        name="out_proj_ln",
    )(x2, attn, mlstm, wa, wm, g[None, :].astype(F32), b[None, :].astype(F32))


def _topk_rows(s, k, payload=None):
    R, N = s.shape
    rid = lax.broadcasted_iota(jnp.int32, (R, N), 0)
    vals, idxs = [], []
    for _ in range(k):
        mx = jnp.max(s, axis=0, keepdims=True)
        ix = jnp.min(jnp.where(s == mx, rid, R), axis=0, keepdims=True)
        sel = rid == ix
        vals.append(mx)
        if payload is None:
            idxs.append(ix)
        else:
            idxs.append(jnp.sum(jnp.where(sel, payload, 0), axis=0, keepdims=True))
        s = jnp.where(sel, -jnp.inf, s)
    return jnp.concatenate(vals, axis=0), jnp.concatenate(idxs, axis=0)


def _peer_route_kernel(h_ref, wq_ref, keys_ref, eidx_ref, gate_ref, *, sub):
    tm = h_ref.shape[0]
    K = PEER_TOPK
    hb = h_ref[...].astype(BF16)
    qry = jnp.dot(hb, wq_ref[...], preferred_element_type=F32).astype(BF16)
    pairs = [(a, b) for a in range(K) for b in range(K) if (a + 1) * (b + 1) <= K]
    for t0 in range(0, tm, sub):
        e_rows, g_rows = [], []
        for h in range(PEER_HEADS):
            tops = []
            for p in range(2):
                c0 = (h * 2 + p) * LANES
                qhp = qry[t0:t0 + sub, c0:c0 + LANES]
                sc = lax.dot_general(keys_ref[h, p], qhp, (((1,), (1,)), ((), ())),
                                     preferred_element_type=F32)
                tops.append(_topk_rows(sc, K))
            (v1, i1), (v2, i2) = tops
            cand = jnp.concatenate([v1[a:a + 1] + v2[b:b + 1] for a, b in pairs], axis=0)
            cidx = jnp.concatenate([i1[a:a + 1] * PEER_N_KEYS + i2[b:b + 1] for a, b in pairs], axis=0)
            best, eidx = _topk_rows(cand, K, payload=cidx)
            ex = jnp.exp(best - best[0:1])
            g_rows.append(ex / jnp.sum(ex, axis=0, keepdims=True))
            e_rows.append(eidx)
        eidx_ref[:, t0:t0 + sub] = jnp.concatenate(e_rows, axis=0)
        gate_ref[:, t0:t0 + sub] = jnp.concatenate(g_rows, axis=0)


def _peer_route(h1, w_query, sub_keys, tm):
    T, D = h1.shape
    NQ = w_query.shape[1]
    R = PEER_HEADS * PEER_TOPK
    return pl.pallas_call(
        functools.partial(_peer_route_kernel, sub=LANES),
        grid=(T // tm,),
        in_specs=[pl.BlockSpec((tm, D), lambda i: (i, 0)),
                  pl.BlockSpec((D, NQ), lambda i: (0, 0)),
                  pl.BlockSpec(sub_keys.shape, lambda i: (0, 0, 0, 0))],
        out_specs=[pl.BlockSpec((R, tm), lambda i: (0, i)), pl.BlockSpec((R, tm), lambda i: (0, i))],
        out_shape=[jax.ShapeDtypeStruct((R, T), jnp.int32), jax.ShapeDtypeStruct((R, T), F32)],
        compiler_params=_cparams("parallel"),
        name="peer_route",
    )(h1, w_query.astype(BF16), sub_keys.astype(BF16))


PEER_TOKENS_PER_STEP = 32
PEER_SLOTS = 8
ROW_TILES = 8


def _gelu_exact(x):
    return 0.5 * x * (1.0 + lax.erf(x * (2.0 ** -0.5)))


def _peer_expert_kernel(idx_ref, h_ref, gate_ref, g_ref, b_ref, table_ref, o_ref, buf, sem, acc, *, nsteps):
    tb, D = h_ref.shape
    R = buf.shape[2]
    half = ROW_TILES
    hi_mask = jnp.uint32(0xFFFF0000)

    def down_f32(w):
        return lax.bitcast_convert_type(w << 16, F32)

    def up_f32(w):
        return lax.bitcast_convert_type(w & hi_mask, F32)
    ahead = PEER_SLOTS - 1
    i = pl.program_id(0)

    def start_row(row, slot, k):
        pltpu.make_async_copy(table_ref.at[idx_ref[0, row, k]], buf.at[slot, :, k, :], sem.at[slot]).start(priority=k % 2)

    def wait(slot):
        pltpu.make_async_copy(buf.at[slot], buf.at[slot], sem.at[slot]).wait()

    @pl.when(i == 0)
    def _():
        for s in range(ahead):
            for k in range(R):
                start_row(s, s, k)

    lane_tok = lax.broadcasted_iota(jnp.int32, (1, tb), 1)
    SUB = 8
    npiece = R // SUB
    per_piece = 3
    mid = (R - 2 * npiece * per_piece) // 2

    def token(j, s):
        islot = (s + ahead) % PEER_SLOTS
        pending = iter(range(R))

        def start_rows(n):
            for _ in range(n):
                start_row(j + ahead, islot, next(pending))

        x = h_ref[pl.ds(j, 1), :]
        xb = [jnp.broadcast_to(x[:, c * LANES:(c + 1) * LANES], (SUB, LANES)) for c in range(half)]
        parts = []
        for p in range(npiece):
            rows = slice(p * SUB, (p + 1) * SUB)
            t = down_f32(buf[s, 0, rows, :]) * xb[0]
            for c in range(1, half):
                t = t + down_f32(buf[s, c, rows, :]) * xb[c]
            parts.append(t)
            start_rows(per_piece)
        a = jnp.sum(jnp.concatenate(parts, axis=0), axis=1, keepdims=True)
        gcol = jnp.sum(jnp.where(lane_tok == j, gate_ref[...], 0.0), axis=1, keepdims=True)
        start_rows(mid)
        coef = jnp.broadcast_to(_gelu_exact(a) * gcol, (R, LANES))
        start_rows(R - 2 * npiece * per_piece - mid)
        outs = [None] * half
        for p in range(npiece):
            rows = slice(p * SUB, (p + 1) * SUB)
            cb = coef[rows, :]
            for c in range(half):
                t = up_f32(buf[s, c, rows, :]) * cb
                outs[c] = t if outs[c] is None else outs[c] + t
            start_rows(per_piece)
        acc[pl.ds(j, 1), :] = jnp.concatenate([jnp.sum(o, axis=0, keepdims=True) for o in outs], axis=1)

    def group(jj, carry):
        for s in range(PEER_SLOTS):
            wait(s)
            token(jj * PEER_SLOTS + s, s)
        return carry

    lax.fori_loop(0, tb // PEER_SLOTS, group, 0)

    @pl.when(i == nsteps - 1)
    def _():
        for s in range(ahead):
            wait(s)

    o_ref[...] = _layer_norm_rows(DEEPNORM_ALPHA * h_ref[...] + acc[...], g_ref[...], b_ref[...])


def _peer_experts(h1, eidx_t, gate_t, w_down, w_up, g, b):
    T, D = h1.shape
    R = eidx_t.shape[0]
    tb = PEER_TOKENS_PER_STEP
    nsteps = T // tb
    ahead = PEER_SLOTS - 1
    assert tb % PEER_SLOTS == 0 and D == ROW_TILES * LANES
    lo = lax.bitcast_convert_type(w_down.astype(BF16), jnp.uint16).astype(jnp.uint32)
    hi = lax.bitcast_convert_type(w_up.astype(BF16), jnp.uint16).astype(jnp.uint32)
    table = (lo | (hi << 16)).reshape(-1, ROW_TILES, LANES)
    eidx = jnp.pad(eidx_t.T, ((0, tb), (0, 0)))
    win = (jnp.arange(nsteps)[:, None] * tb + jnp.arange(tb + ahead)[None, :])
    eidx_win = eidx[win]
    gate_blk = gate_t.reshape(R, nsteps, tb).transpose(1, 0, 2)
    return pl.pallas_call(
        functools.partial(_peer_expert_kernel, nsteps=nsteps),
        grid=(nsteps,),
        in_specs=[pl.BlockSpec((1, tb + ahead, R), lambda i: (i, 0, 0), memory_space=pltpu.SMEM),
                  pl.BlockSpec((tb, D), lambda i: (i, 0)),
                  pl.BlockSpec((None, R, tb), lambda i: (i, 0, 0)),
                  pl.BlockSpec((1, D), lambda i: (0, 0)),
                  pl.BlockSpec((1, D), lambda i: (0, 0)),
                  pl.BlockSpec(memory_space=pl.ANY)],
        out_specs=pl.BlockSpec((tb, D), lambda i: (i, 0)),
        out_shape=jax.ShapeDtypeStruct((T, D), F32),
        scratch_shapes=[pltpu.VMEM((PEER_SLOTS, ROW_TILES, R, LANES), jnp.uint32),
                        pltpu.SemaphoreType.DMA((PEER_SLOTS,)),
                        pltpu.VMEM((tb, D), F32)],
        compiler_params=_cparams("arbitrary"),
        name="peer_experts",
    )(eidx_win, h1, gate_blk, g[None, :].astype(F32), b[None, :].astype(F32), table)


def kernel(x, w_in, conv_w, conv_b, w_mq, w_mk, b_igate, b_fgate, mh_norm_g, w_out, ln1_g, ln1_b,
           peer_w_query, peer_sub_keys, peer_w_down, peer_w_up, ln2_g, ln2_b):
    B, S, D = x.shape
    T = B * S
    H = MLSTM_HEADS
    h = x.reshape(T, D)
    for l in range(DEPTH):
        n_gate = 2 * H
        w_main = w_in[l][:, :-n_gate].astype(BF16)
        w_gate = jnp.pad(w_in[l][:, -n_gate:], ((0, 0), (0, LANES - n_gate))).astype(BF16)
        q_a, k_a, v_a, x_m, v_m, o_m, gates = _in_proj(h, w_main, w_gate, tm=min(512, T))
        attn = _dilated_attention(q_a, k_a, v_a, B, S)
        mlstm = _mlstm(x_m, v_m, o_m, gates[:, :n_gate], conv_w[l], conv_b[l], w_mq[l], w_mk[l],
                       b_igate[l], b_fgate[l], mh_norm_g[l], B, S)
        h1 = _out_proj_ln(h, attn, mlstm, w_out[l], ln1_g[l], ln1_b[l], tm=min(512, T))
        eidx_t, gate_t = _peer_route(h1, peer_w_query[l], peer_sub_keys[l], tm=min(512, T))
        h = _peer_experts(h1, eidx_t, gate_t, peer_w_down[l], peer_w_up[l], ln2_g[l], ln2_b[l])
    return h.reshape(B, S, D)
```

```python
import functools
import math

import numpy as np
import jax
import jax.numpy as jnp
from jax import lax
from jax.experimental import pallas as pl
from jax.experimental.pallas import tpu as pltpu

F32 = jnp.float32
BF16 = jnp.bfloat16

ATTN_HEAD_DIM = 64
DILATED_PATTERNS = ((128, 1), (512, 4), (2048, 16))
MLSTM_HEADS = 4
MLSTM_CONV = 4
MLSTM_CHUNK = 128
PEER_HEADS = 8
PEER_N_KEYS = 128
PEER_TOPK = 16
DEPTH = 1
DEEPNORM_ALPHA = (2.0 * DEPTH) ** 0.25
LN_EPS = 1e-5

LANES = 128
NEG_BIG = -1e30
VMEM_LIMIT = 56 * 1024 * 1024


def _cparams(*sem):
    return pltpu.CompilerParams(dimension_semantics=sem, vmem_limit_bytes=VMEM_LIMIT)


def _in_proj_kernel(x_ref, w_ref, wg_ref, *out_refs):
    xb = x_ref[...].astype(BF16)
    g_ref = out_refs[-1]
    for i, o_ref in enumerate(out_refs[:-1]):
        n = o_ref.shape[1]
        o_ref[...] = jnp.dot(xb, w_ref[:, i * n:(i + 1) * n], preferred_element_type=F32).astype(o_ref.dtype)
    g_ref[...] = jnp.dot(xb, wg_ref[...], preferred_element_type=F32)


def _in_proj(x2, w_main, w_gate, tm):
    T, D = x2.shape
    n_main = w_main.shape[1]
    n_out = 6
    wcol = n_main // n_out
    out_shape = [jax.ShapeDtypeStruct((T, wcol), BF16) for _ in range(n_out)]
    out_shape.append(jax.ShapeDtypeStruct((T, LANES), F32))
    out_specs = [pl.BlockSpec((tm, wcol), lambda i: (i, 0)) for _ in range(n_out)]
    out_specs.append(pl.BlockSpec((tm, LANES), lambda i: (i, 0)))
    return pl.pallas_call(
        _in_proj_kernel,
        grid=(T // tm,),
        in_specs=[pl.BlockSpec((tm, D), lambda i: (i, 0)),
                  pl.BlockSpec((D, n_main), lambda i: (0, 0)),
                  pl.BlockSpec((D, LANES), lambda i: (0, 0))],
        out_specs=out_specs,
        out_shape=out_shape,
        compiler_params=_cparams("parallel"),
        name="in_proj",
    )(x2, w_main, w_gate)


def _attn_bias_table(S, blk):
    nrel = S // blk
    rel = np.arange(nrel)[:, None, None] * blk
    delta = rel + np.arange(blk)[None, None, :] - np.arange(blk)[None, :, None]
    cnt = np.zeros(delta.shape, np.float64)
    for window, dil in DILATED_PATTERNS:
        cnt += (delta >= 0) & (delta <= window) & (delta % dil == 0)
    bias = np.where(cnt > 0, np.log(np.maximum(cnt, 1.0)), NEG_BIG)
    return jnp.asarray(bias, F32)


def _attn_kernel(q_ref, k_ref, vt_ref, bias_ref, o_ref, *, blk, scale):
    S = q_ref.shape[0]
    nblk = S // blk
    lo = lax.broadcasted_iota(jnp.int32, (blk, LANES), 1) < ATTN_HEAD_DIM
    row_lo = lax.broadcasted_iota(jnp.int32, (LANES, blk), 0) < ATTN_HEAD_DIM

    def q_block(i, carry):
        r0 = pl.multiple_of(i * blk, blk)
        q = q_ref[pl.ds(r0, blk), :]
        qs = (q.astype(F32) * scale).astype(BF16)
        zero = jnp.zeros_like(qs)
        q2 = jnp.concatenate([jnp.where(lo, qs, zero), jnp.where(lo, zero, qs)], axis=0)

        def k_block(j, st):
            m, l, acc = st
            c0 = pl.multiple_of(j * blk, blk)
            kt = k_ref[pl.ds(c0, blk), :]
            s = lax.dot_general(kt, q2, (((1,), (1,)), ((), ())), preferred_element_type=F32)
            b = bias_ref[i - j]
            s = s + jnp.concatenate([b, b], axis=1)
            m_new = jnp.maximum(m, jnp.max(s, axis=0, keepdims=True))
            alpha = jnp.exp(m - m_new)
            p = jnp.exp(s - m_new)
            l = alpha * l + jnp.sum(p, axis=0, keepdims=True)
            acc = alpha * acc + jnp.dot(vt_ref[j], p.astype(BF16), preferred_element_type=F32)
            return m_new, l, acc

        init = (jnp.full((1, 2 * blk), -jnp.inf, F32), jnp.zeros((1, 2 * blk), F32),
                jnp.zeros((LANES, 2 * blk), F32))
        m, l, acc = lax.fori_loop(0, i + 1, k_block, init)
        out = acc / l
        out_t = jnp.where(row_lo, out[:, :blk], out[:, blk:])
        o_ref[pl.ds(r0, blk), :] = out_t.T.astype(o_ref.dtype)
        return carry

    lax.fori_loop(0, nblk, q_block, 0)


def _dilated_attention(q, k, v, B, S):
    T, W = q.shape
    blk = min(256, S)
    nblk = S // blk
    bias = _attn_bias_table(S, blk)
    npair = W // LANES
    vt = v.reshape(B, nblk, blk, npair, LANES).transpose(0, 3, 1, 4, 2)
    spec = pl.BlockSpec((S, LANES), lambda b, h: (b, h))
    return pl.pallas_call(
        functools.partial(_attn_kernel, blk=blk, scale=ATTN_HEAD_DIM ** -0.5),
        grid=(B, npair),
        in_specs=[spec, spec,
                  pl.BlockSpec((None, None, nblk, LANES, blk), lambda b, h: (b, h, 0, 0, 0)),
                  pl.BlockSpec(bias.shape, lambda b, h: (0, 0, 0))],
        out_specs=spec,
        out_shape=jax.ShapeDtypeStruct((T, W), BF16),
        compiler_params=_cparams("parallel", "parallel"),
        name="dilated_attn",
    )(q, k, vt, bias)


def _log_sigmoid(x):
    return jnp.minimum(x, 0.0) - jnp.log1p(jnp.exp(-jnp.abs(x)))


def _mlstm_kernel(xm_ref, vm_ref, om_ref, gcol_ref, grow_ref, gb_row_ref, gb_col_ref, cw_ref, cb_ref,
                  wq_ref, wk_ref, ng_ref, o_ref, q_s, k_s, c_s):
    S, W = xm_ref.shape
    H = MLSTM_HEADS
    E = W // H
    CH = MLSTM_CHUNK
    NC = S // CH
    hi = lax.Precision.HIGHEST

    x32 = xm_ref[...].astype(F32)
    row = lax.broadcasted_iota(jnp.int32, (S, W), 0)
    xc = x32 * cw_ref[MLSTM_CONV - 1:MLSTM_CONV, :] + cb_ref[...]
    for j in range(1, MLSTM_CONV):
        sh = jnp.where(row >= j, pltpu.roll(x32, j, axis=0), 0.0)
        xc = xc + sh * cw_ref[MLSTM_CONV - 1 - j:MLSTM_CONV - j, :]
    xc = (xc * jax.nn.sigmoid(xc)).astype(BF16)
    for h in range(H):
        xh = xc[:, h * E:(h + 1) * E]
        q_s[h] = (jnp.dot(xh, wq_ref[h], preferred_element_type=F32) * (E ** -0.5)).astype(BF16)
        k_s[h] = jnp.dot(xh, wk_ref[h], preferred_element_type=F32).astype(BF16)

    c_s[...] = jnp.zeros_like(c_s)
    ri = lax.broadcasted_iota(jnp.int32, (CH, CH), 0)
    ci = lax.broadcasted_iota(jnp.int32, (CH, CH), 1)
    tril = ri >= ci
    tril_f = tril.astype(F32)
    triu_f = (ri <= ci).astype(F32)

    def chunk(c, carry):
        r0 = pl.multiple_of(c * CH, CH)
        gc = gcol_ref[c] + gb_row_ref[...]
        gr = grow_ref[c] + gb_col_ref[...]
        bc_col = jnp.dot(tril_f, _log_sigmoid(gc), precision=hi, preferred_element_type=F32)
        bc_row = jnp.dot(_log_sigmoid(gr), triu_f, precision=hi, preferred_element_type=F32)
        new_carry = []
        for h in range(H):
            n_prev, m_prev = carry[h]
            bcol = bc_col[:, H + h:H + h + 1]
            brow = bc_row[H + h:H + h + 1, :]
            icol = gc[:, h:h + 1]
            irow = gr[h:h + 1, :]
            qc = q_s[h, pl.ds(r0, CH), :]
            kc = k_s[h, pl.ds(r0, CH), :]
            vc = vm_ref[pl.ds(r0, CH), h * E:(h + 1) * E]
            C = c_s[h]

            dlog = jnp.where(tril, bcol - brow + irow, -jnp.inf)
            m_inter = bcol + m_prev
            m_t = jnp.maximum(m_inter, jnp.max(dlog, axis=1, keepdims=True))
            w_inter = jnp.exp(m_inter - m_t)
            qk = lax.dot_general(qc, kc, (((1,), (1,)), ((), ())), preferred_element_type=F32)
            P = jnp.exp(dlog - m_t) * qk
            inter = lax.dot_general(qc, C.astype(BF16), (((1,), (1,)), ((), ())), preferred_element_type=F32)
            num = w_inter * inter + jnp.dot(P.astype(BF16), vc, preferred_element_type=F32)
            qn = jnp.sum(qc.astype(F32) * n_prev, axis=1, keepdims=True)
            den = w_inter * qn + jnp.sum(P, axis=1, keepdims=True)
            ht = num / jnp.maximum(jnp.abs(den), jnp.exp(-m_t))

            bL = bcol[CH - 1:CH, :]
            g_row = bL - brow + irow
            g_col = bL - bcol + icol
            m_new = jnp.maximum(bL + m_prev, jnp.max(g_row, axis=1, keepdims=True))
            a = jnp.exp(bL + m_prev - m_new)
            wk_col = jnp.exp(g_col - m_new)
            vw = (vc.astype(F32) * wk_col).astype(BF16)
            c_s[h] = a * C + lax.dot_general(vw, kc, (((0,), (0,)), ((), ())), preferred_element_type=F32)
            n_new = a * n_prev + jnp.sum(kc.astype(F32) * wk_col, axis=0, keepdims=True)
            new_carry.append((n_new, m_new))

            mu = jnp.mean(ht, axis=1, keepdims=True)
            var = jnp.mean(jnp.square(ht - mu), axis=1, keepdims=True)
            hn = (ht - mu) * lax.rsqrt(var + LN_EPS) * ng_ref[:, h * E:(h + 1) * E]
            og = om_ref[pl.ds(r0, CH), h * E:(h + 1) * E].astype(F32)
            o_ref[pl.ds(r0, CH), h * E:(h + 1) * E] = (jax.nn.sigmoid(og) * hn).astype(o_ref.dtype)
        return tuple(new_carry)

    init = tuple((jnp.zeros((1, E), F32), jnp.zeros((1, 1), F32)) for _ in range(H))
    lax.fori_loop(0, NC, chunk, init)


def _mlstm(xm, vm, om, gates, conv_w, conv_b, w_mq, w_mk, b_igate, b_fgate, mh_norm_g, B, S):
    T, W = xm.shape
    H = MLSTM_HEADS
    E = W // H
    CH = MLSTM_CHUNK
    NC = S // CH
    gcol = gates.reshape(B, NC, CH, 2 * H)
    grow = gcol.transpose(0, 1, 3, 2)
    gb = jnp.concatenate([b_igate, b_fgate]).astype(F32)
    seq = pl.BlockSpec((S, W), lambda b: (b, 0))
    full2 = lambda shape: pl.BlockSpec(shape, lambda b: (0,) * len(shape))
    return pl.pallas_call(
        _mlstm_kernel,
        grid=(B,),
        in_specs=[seq, seq, seq,
                  pl.BlockSpec((None, NC, CH, 2 * H), lambda b: (b, 0, 0, 0)),
                  pl.BlockSpec((None, NC, 2 * H, CH), lambda b: (b, 0, 0, 0)),
                  full2((1, 2 * H)), full2((2 * H, 1)), full2((MLSTM_CONV, W)), full2((1, W)),
                  full2((H, E, E)), full2((H, E, E)), full2((1, W))],
        out_specs=seq,
        out_shape=jax.ShapeDtypeStruct((T, W), BF16),
        scratch_shapes=[pltpu.VMEM((H, S, E), BF16), pltpu.VMEM((H, S, E), BF16), pltpu.VMEM((H, E, E), F32)],
        compiler_params=_cparams("parallel"),
        name="mlstm",
    )(xm, vm, om, gcol, grow, gb[None, :], gb[:, None], conv_w.astype(F32), conv_b[None, :].astype(F32),
      w_mq.astype(BF16), w_mk.astype(BF16), mh_norm_g[None, :].astype(F32))


def _layer_norm_rows(y, g, b):
    mu = jnp.mean(y, axis=1, keepdims=True)
    d = y - mu
    var = jnp.mean(d * d, axis=1, keepdims=True)
    return d * lax.rsqrt(var + LN_EPS) * g + b


def _out_proj_kernel(x_ref, a_ref, m_ref, wa_ref, wm_ref, g_ref, b_ref, o_ref):
    mix = jnp.dot(a_ref[...], wa_ref[...], preferred_element_type=F32)
    mix = mix + jnp.dot(m_ref[...], wm_ref[...], preferred_element_type=F32)
    o_ref[...] = _layer_norm_rows(DEEPNORM_ALPHA * x_ref[...] + mix, g_ref[...], b_ref[...])


def _out_proj_ln(x2, attn, mlstm, w_out, g, b, tm):
    T, D = x2.shape
    Wa = attn.shape[1]
    Wm = mlstm.shape[1]
    wa = w_out[:Wa].astype(BF16)
    wm = w_out[Wa:].astype(BF16)
    return pl.pallas_call(
        _out_proj_kernel,
        grid=(T // tm,),
        in_specs=[pl.BlockSpec((tm, D), lambda i: (i, 0)),
                  pl.BlockSpec((tm, Wa), lambda i: (i, 0)),
                  pl.BlockSpec((tm, Wm), lambda i: (i, 0)),
                  pl.BlockSpec((Wa, D), lambda i: (0, 0)),
                  pl.BlockSpec((Wm, D), lambda i: (0, 0)),
                  pl.BlockSpec((1, D), lambda i: (0, 0)),
                  pl.BlockSpec((1, D), lambda i: (0, 0))],
        out_specs=pl.BlockSpec((tm, D), lambda i: (i, 0)),
        out_shape=jax.ShapeDtypeStruct((T, D), F32),
        compiler_params=_cparams("parallel"),
        name="out_proj_ln",
    )(x2, attn, mlstm, wa, wm, g[None, :].astype(F32), b[None, :].astype(F32))


def _topk_rows(s, k, payload=None):
    R, N = s.shape
    rid = lax.broadcasted_iota(jnp.int32, (R, N), 0).astype(F32)
    vals, idxs = [], []
    for _ in range(k):
        mx = jnp.max(s, axis=0, keepdims=True)
        ix = jnp.min(jnp.where(s == mx, rid, float(R)), axis=0, keepdims=True)
        sel = rid == ix
        vals.append(mx)
        if payload is None:
            idxs.append(ix)
        else:
            idxs.append(jnp.sum(jnp.where(sel, payload, 0.0), axis=0, keepdims=True))
        s = jnp.where(sel, -jnp.inf, s)
    return jnp.concatenate(vals, axis=0), jnp.concatenate(idxs, axis=0)


def _peer_route_kernel(h_ref, wq_ref, keys_ref, eidx_ref, gate_ref, *, sub):
    tm = h_ref.shape[0]
    K = PEER_TOPK
    hb = h_ref[...].astype(BF16)
    qry = jnp.dot(hb, wq_ref[...], preferred_element_type=F32).astype(BF16)
    pairs = [(a, b) for a in range(K) for b in range(K) if (a + 1) * (b + 1) <= K]
    for t0 in range(0, tm, sub):
        e_rows, g_rows = [], []
        for h in range(PEER_HEADS):
            tops = []
            for p in range(2):
                c0 = (h * 2 + p) * LANES
                qhp = qry[t0:t0 + sub, c0:c0 + LANES]
                sc = lax.dot_general(keys_ref[h, p], qhp, (((1,), (1,)), ((), ())),
                                     preferred_element_type=F32)
                tops.append(_topk_rows(sc, K))
            (v1, i1), (v2, i2) = tops
            cand = jnp.concatenate([v1[a:a + 1] + v2[b:b + 1] for a, b in pairs], axis=0)
            cidx = jnp.concatenate([i1[a:a + 1] * PEER_N_KEYS + i2[b:b + 1] for a, b in pairs], axis=0)
            best, eidx = _topk_rows(cand, K, payload=cidx)
            ex = jnp.exp(best - best[0:1])
            g_rows.append(ex / jnp.sum(ex, axis=0, keepdims=True))
            e_rows.append(eidx.astype(jnp.int32))
        eidx_ref[:, t0:t0 + sub] = jnp.concatenate(e_rows, axis=0)
        gate_ref[:, t0:t0 + sub] = jnp.concatenate(g_rows, axis=0)


def _peer_route(h1, w_query, sub_keys, tm):
    T, D = h1.shape
    NQ = w_query.shape[1]
    R = PEER_HEADS * PEER_TOPK
    return pl.pallas_call(
        functools.partial(_peer_route_kernel, sub=LANES),
        grid=(T // tm,),
        in_specs=[pl.BlockSpec((tm, D), lambda i: (i, 0)),
                  pl.BlockSpec((D, NQ), lambda i: (0, 0)),
                  pl.BlockSpec(sub_keys.shape, lambda i: (0, 0, 0, 0))],
        out_specs=[pl.BlockSpec((R, tm), lambda i: (0, i)), pl.BlockSpec((R, tm), lambda i: (0, i))],
        out_shape=[jax.ShapeDtypeStruct((R, T), jnp.int32), jax.ShapeDtypeStruct((R, T), F32)],
        compiler_params=_cparams("parallel"),
        name="peer_route",
    )(h1, w_query.astype(BF16), sub_keys.astype(BF16))


PEER_TOKENS_PER_STEP = 128
PEER_SLOTS = 8
LANE_TILES = 16


def _gelu_exact(x):
    return 0.5 * x * (1.0 + lax.erf(x * (2.0 ** -0.5)))


def _peer_expert_kernel(idx_ref, h_ref, gate_ref, g_ref, b_ref, table_ref, o_ref, buf, sem, acc, *, nsteps):
    tb, D = h_ref.shape
    R = buf.shape[2]
    half = LANE_TILES // 2
    ahead = PEER_SLOTS - 1
    i = pl.program_id(0)

    def start_row(row, slot, k):
        pltpu.make_async_copy(table_ref.at[idx_ref[0, row, k]], buf.at[slot, :, k, :], sem.at[slot]).start(priority=k % 2)

    def wait(slot):
        pltpu.make_async_copy(buf.at[slot], buf.at[slot], sem.at[slot]).wait()

    @pl.when(i == 0)
    def _():
        for s in range(ahead):
            for k in range(R):
                start_row(s, s, k)

    lane_tok = lax.broadcasted_iota(jnp.int32, (1, tb), 1)
    SUB = 8
    npiece = R // SUB
    per_piece = 3
    mid = (R - 2 * npiece * per_piece) // 2

    def token(j, s):
        islot = (s + ahead) % PEER_SLOTS
        pending = iter(range(R))

        def start_rows(n):
            for _ in range(n):
                start_row(j + ahead, islot, next(pending))

        x = h_ref[pl.ds(j, 1), :]
        xb = [jnp.broadcast_to(x[:, c * LANES:(c + 1) * LANES], (SUB, LANES)) for c in range(half)]
        parts = []
        for p in range(npiece):
            rows = slice(p * SUB, (p + 1) * SUB)
            t = buf[s, 0, rows, :] * xb[0]
            for c in range(1, half):
                t = t + buf[s, c, rows, :] * xb[c]
            parts.append(t)
            start_rows(per_piece)
        a = jnp.sum(jnp.concatenate(parts, axis=0), axis=1, keepdims=True)
        gcol = jnp.sum(jnp.where(lane_tok == j, gate_ref[...], 0.0), axis=1, keepdims=True)
        start_rows(mid)
        coef = jnp.broadcast_to(_gelu_exact(a) * gcol, (R, LANES))
        start_rows(R - 2 * npiece * per_piece - mid)
        outs = [None] * half
        for p in range(npiece):
            rows = slice(p * SUB, (p + 1) * SUB)
            cb = coef[rows, :]
            for c in range(half):
                t = buf[s, half + c, rows, :] * cb
                outs[c] = t if outs[c] is None else outs[c] + t
            start_rows(per_piece)
        acc[pl.ds(j, 1), :] = jnp.concatenate([jnp.sum(o, axis=0, keepdims=True) for o in outs], axis=1)

    def group(jj, carry):
        for s in range(PEER_SLOTS):
            wait(s)
            token(jj * PEER_SLOTS + s, s)
        return carry

    lax.fori_loop(0, tb // PEER_SLOTS, group, 0)

    @pl.when(i == nsteps - 1)
    def _():
        for s in range(ahead):
            wait(s)

    o_ref[...] = _layer_norm_rows(DEEPNORM_ALPHA * h_ref[...] + acc[...], g_ref[...], b_ref[...])


def _peer_experts(h1, eidx_t, gate_t, w_down, w_up, g, b):
    T, D = h1.shape
    R = eidx_t.shape[0]
    tb = PEER_TOKENS_PER_STEP
    nsteps = T // tb
    ahead = PEER_SLOTS - 1
    assert tb % PEER_SLOTS == 0 and 2 * D == LANE_TILES * LANES
    table = jnp.concatenate([w_down, w_up], axis=1).reshape(-1, LANE_TILES, LANES)
    eidx = jnp.pad(eidx_t.T, ((0, tb), (0, 0)))
    win = (jnp.arange(nsteps)[:, None] * tb + jnp.arange(tb + ahead)[None, :])
    eidx_win = eidx[win]
    gate_blk = gate_t.reshape(R, nsteps, tb).transpose(1, 0, 2)
    return pl.pallas_call(
        functools.partial(_peer_expert_kernel, nsteps=nsteps),
        grid=(nsteps,),
        in_specs=[pl.BlockSpec((1, tb + ahead, R), lambda i: (i, 0, 0), memory_space=pltpu.SMEM),
                  pl.BlockSpec((tb, D), lambda i: (i, 0)),
                  pl.BlockSpec((None, R, tb), lambda i: (i, 0, 0)),
                  pl.BlockSpec((1, D), lambda i: (0, 0)),
                  pl.BlockSpec((1, D), lambda i: (0, 0)),
                  pl.BlockSpec(memory_space=pl.ANY)],
        out_specs=pl.BlockSpec((tb, D), lambda i: (i, 0)),
        out_shape=jax.ShapeDtypeStruct((T, D), F32),
        scratch_shapes=[pltpu.VMEM((PEER_SLOTS, LANE_TILES, R, LANES), F32),
                        pltpu.SemaphoreType.DMA((PEER_SLOTS,)),
                        pltpu.VMEM((tb, D), F32)],
        compiler_params=_cparams("arbitrary"),
        name="peer_experts",
    )(eidx_win, h1, gate_blk, g[None, :].astype(F32), b[None, :].astype(F32), table)


def kernel(x, w_in, conv_w, conv_b, w_mq, w_mk, b_igate, b_fgate, mh_norm_g, w_out, ln1_g, ln1_b,
           peer_w_query, peer_sub_keys, peer_w_down, peer_w_up, ln2_g, ln2_b):
    B, S, D = x.shape
    T = B * S
    H = MLSTM_HEADS
    h = x.reshape(T, D)
    for l in range(DEPTH):
        n_gate = 2 * H
        w_main = w_in[l][:, :-n_gate].astype(BF16)
        w_gate = jnp.pad(w_in[l][:, -n_gate:], ((0, 0), (0, LANES - n_gate))).astype(BF16)
        q_a, k_a, v_a, x_m, v_m, o_m, gates = _in_proj(h, w_main, w_gate, tm=min(512, T))
        attn = _dilated_attention(q_a, k_a, v_a, B, S)
        mlstm = _mlstm(x_m, v_m, o_m, gates[:, :n_gate], conv_w[l], conv_b[l], w_mq[l], w_mk[l],
                       b_igate[l], b_fgate[l], mh_norm_g[l], B, S)
        h1 = _out_proj_ln(h, attn, mlstm, w_out[l], ln1_g[l], ln1_b[l], tm=min(512, T))
        eidx_t, gate_t = _peer_route(h1, peer_w_query[l], peer_sub_keys[l], tm=min(512, T))
        h = _peer_experts(h1, eidx_t, gate_t, peer_w_down[l], peer_w_up[l], ln2_g[l], ln2_b[l])
    return h.reshape(B, S, D)
```

```python
import functools
import math

import numpy as np
import jax
import jax.numpy as jnp
from jax import lax
from jax.experimental import pallas as pl
from jax.experimental.pallas import tpu as pltpu

F32 = jnp.float32
BF16 = jnp.bfloat16

ATTN_HEAD_DIM = 64
DILATED_PATTERNS = ((128, 1), (512, 4), (2048, 16))
MLSTM_HEADS = 4
MLSTM_CONV = 4
MLSTM_CHUNK = 128
PEER_HEADS = 8
PEER_N_KEYS = 128
PEER_TOPK = 16
DEPTH = 1
DEEPNORM_ALPHA = (2.0 * DEPTH) ** 0.25
LN_EPS = 1e-5

LANES = 128
NEG_BIG = -1e30
VMEM_LIMIT = 56 * 1024 * 1024


def _cparams(*sem):
    return pltpu.CompilerParams(dimension_semantics=sem, vmem_limit_bytes=VMEM_LIMIT)


def _in_proj_kernel(x_ref, w_ref, wg_ref, *out_refs):
    xb = x_ref[...].astype(BF16)
    g_ref = out_refs[-1]
    for i, o_ref in enumerate(out_refs[:-1]):
        n = o_ref.shape[1]
        o_ref[...] = jnp.dot(xb, w_ref[:, i * n:(i + 1) * n], preferred_element_type=F32).astype(o_ref.dtype)
    g_ref[...] = jnp.dot(xb, wg_ref[...], preferred_element_type=F32)


def _in_proj(x2, w_main, w_gate, tm):
    T, D = x2.shape
    n_main = w_main.shape[1]
    n_out = 6
    wcol = n_main // n_out
    out_shape = [jax.ShapeDtypeStruct((T, wcol), BF16) for _ in range(n_out)]
    out_shape.append(jax.ShapeDtypeStruct((T, LANES), F32))
    out_specs = [pl.BlockSpec((tm, wcol), lambda i: (i, 0)) for _ in range(n_out)]
    out_specs.append(pl.BlockSpec((tm, LANES), lambda i: (i, 0)))
    return pl.pallas_call(
        _in_proj_kernel,
        grid=(T // tm,),
        in_specs=[pl.BlockSpec((tm, D), lambda i: (i, 0)),
                  pl.BlockSpec((D, n_main), lambda i: (0, 0)),
                  pl.BlockSpec((D, LANES), lambda i: (0, 0))],
        out_specs=out_specs,
        out_shape=out_shape,
        compiler_params=_cparams("parallel"),
        name="in_proj",
    )(x2, w_main, w_gate)


def _attn_bias_table(S, blk):
    nrel = S // blk
    rel = np.arange(nrel)[:, None, None] * blk
    delta = rel + np.arange(blk)[None, None, :] - np.arange(blk)[None, :, None]
    cnt = np.zeros(delta.shape, np.float64)
    for window, dil in DILATED_PATTERNS:
        cnt += (delta >= 0) & (delta <= window) & (delta % dil == 0)
    bias = np.where(cnt > 0, np.log(np.maximum(cnt, 1.0)), NEG_BIG)
    return jnp.asarray(bias, F32)


def _attn_kernel(q_ref, k_ref, vt_ref, bias_ref, o_ref, *, blk, scale):
    S = q_ref.shape[0]
    nblk = S // blk
    lo = lax.broadcasted_iota(jnp.int32, (blk, LANES), 1) < ATTN_HEAD_DIM
    row_lo = lax.broadcasted_iota(jnp.int32, (LANES, blk), 0) < ATTN_HEAD_DIM

    def q_block(i, carry):
        r0 = pl.multiple_of(i * blk, blk)
        q = q_ref[pl.ds(r0, blk), :]
        qs = (q.astype(F32) * scale).astype(BF16)
        zero = jnp.zeros_like(qs)
        q2 = jnp.concatenate([jnp.where(lo, qs, zero), jnp.where(lo, zero, qs)], axis=0)

        def k_block(j, st):
            m, l, acc = st
            c0 = pl.multiple_of(j * blk, blk)
            kt = k_ref[pl.ds(c0, blk), :]
            s = lax.dot_general(kt, q2, (((1,), (1,)), ((), ())), preferred_element_type=F32)
            b = bias_ref[i - j]
            s = s + jnp.concatenate([b, b], axis=1)
            m_new = jnp.maximum(m, jnp.max(s, axis=0, keepdims=True))
            alpha = jnp.exp(m - m_new)
            p = jnp.exp(s - m_new)
            l = alpha * l + jnp.sum(p, axis=0, keepdims=True)
            acc = alpha * acc + jnp.dot(vt_ref[j], p.astype(BF16), preferred_element_type=F32)
            return m_new, l, acc

        init = (jnp.full((1, 2 * blk), -jnp.inf, F32), jnp.zeros((1, 2 * blk), F32),
                jnp.zeros((LANES, 2 * blk), F32))
        m, l, acc = lax.fori_loop(0, i + 1, k_block, init)
        out = acc / l
        out_t = jnp.where(row_lo, out[:, :blk], out[:, blk:])
        o_ref[pl.ds(r0, blk), :] = out_t.T.astype(o_ref.dtype)
        return carry

    lax.fori_loop(0, nblk, q_block, 0)


def _dilated_attention(q, k, v, B, S):
    T, W = q.shape
    blk = min(256, S)
    nblk = S // blk
    bias = _attn_bias_table(S, blk)
    npair = W // LANES
    vt = v.reshape(B, nblk, blk, npair, LANES).transpose(0, 3, 1, 4, 2)
    spec = pl.BlockSpec((S, LANES), lambda b, h: (b, h))
    return pl.pallas_call(
        functools.partial(_attn_kernel, blk=blk, scale=ATTN_HEAD_DIM ** -0.5),
        grid=(B, npair),
        in_specs=[spec, spec,
                  pl.BlockSpec((None, None, nblk, LANES, blk), lambda b, h: (b, h, 0, 0, 0)),
                  pl.BlockSpec(bias.shape, lambda b, h: (0, 0, 0))],
        out_specs=spec,
        out_shape=jax.ShapeDtypeStruct((T, W), BF16),
        compiler_params=_cparams("parallel", "parallel"),
        name="dilated_attn",
    )(q, k, vt, bias)


def _log_sigmoid(x):
    return jnp.minimum(x, 0.0) - jnp.log1p(jnp.exp(-jnp.abs(x)))


def _mlstm_kernel(xm_ref, vm_ref, om_ref, gcol_ref, grow_ref, gb_row_ref, gb_col_ref, cw_ref, cb_ref,
                  wq_ref, wk_ref, ng_ref, o_ref, q_s, k_s, c_s):
    S, W = xm_ref.shape
    H = MLSTM_HEADS
    E = W // H
    CH = MLSTM_CHUNK
    NC = S // CH
    hi = lax.Precision.HIGHEST

    x32 = xm_ref[...].astype(F32)
    row = lax.broadcasted_iota(jnp.int32, (S, W), 0)
    xc = x32 * cw_ref[MLSTM_CONV - 1:MLSTM_CONV, :] + cb_ref[...]
    for j in range(1, MLSTM_CONV):
        sh = jnp.where(row >= j, pltpu.roll(x32, j, axis=0), 0.0)
        xc = xc + sh * cw_ref[MLSTM_CONV - 1 - j:MLSTM_CONV - j, :]
    xc = (xc * jax.nn.sigmoid(xc)).astype(BF16)
    for h in range(H):
        xh = xc[:, h * E:(h + 1) * E]
        q_s[h] = (jnp.dot(xh, wq_ref[h], preferred_element_type=F32) * (E ** -0.5)).astype(BF16)
        k_s[h] = jnp.dot(xh, wk_ref[h], preferred_element_type=F32).astype(BF16)

    c_s[...] = jnp.zeros_like(c_s)
    ri = lax.broadcasted_iota(jnp.int32, (CH, CH), 0)
    ci = lax.broadcasted_iota(jnp.int32, (CH, CH), 1)
    tril = ri >= ci
    tril_f = tril.astype(F32)
    triu_f = (ri <= ci).astype(F32)

    def chunk(c, carry):
        r0 = pl.multiple_of(c * CH, CH)
        gc = gcol_ref[c] + gb_row_ref[...]
        gr = grow_ref[c] + gb_col_ref[...]
        bc_col = jnp.dot(tril_f, _log_sigmoid(gc), precision=hi, preferred_element_type=F32)
        bc_row = jnp.dot(_log_sigmoid(gr), triu_f, precision=hi, preferred_element_type=F32)
        new_carry = []
        for h in range(H):
            n_prev, m_prev = carry[h]
            bcol = bc_col[:, H + h:H + h + 1]
            brow = bc_row[H + h:H + h + 1, :]
            icol = gc[:, h:h + 1]
            irow = gr[h:h + 1, :]
            qc = q_s[h, pl.ds(r0, CH), :]
            kc = k_s[h, pl.ds(r0, CH), :]
            vc = vm_ref[pl.ds(r0, CH), h * E:(h + 1) * E]
            C = c_s[h]

            dlog = jnp.where(tril, bcol - brow + irow, -jnp.inf)
            m_inter = bcol + m_prev
            m_t = jnp.maximum(m_inter, jnp.max(dlog, axis=1, keepdims=True))
            w_inter = jnp.exp(m_inter - m_t)
            qk = lax.dot_general(qc, kc, (((1,), (1,)), ((), ())), preferred_element_type=F32)
            P = jnp.exp(dlog - m_t) * qk
            inter = lax.dot_general(qc, C.astype(BF16), (((1,), (1,)), ((), ())), preferred_element_type=F32)
            num = w_inter * inter + jnp.dot(P.astype(BF16), vc, preferred_element_type=F32)
            qn = jnp.sum(qc.astype(F32) * n_prev, axis=1, keepdims=True)
            den = w_inter * qn + jnp.sum(P, axis=1, keepdims=True)
            ht = num / jnp.maximum(jnp.abs(den), jnp.exp(-m_t))

            bL = bcol[CH - 1:CH, :]
            g_row = bL - brow + irow
            g_col = bL - bcol + icol
            m_new = jnp.maximum(bL + m_prev, jnp.max(g_row, axis=1, keepdims=True))
            a = jnp.exp(bL + m_prev - m_new)
            wk_col = jnp.exp(g_col - m_new)
            vw = (vc.astype(F32) * wk_col).astype(BF16)
            c_s[h] = a * C + lax.dot_general(vw, kc, (((0,), (0,)), ((), ())), preferred_element_type=F32)
            n_new = a * n_prev + jnp.sum(kc.astype(F32) * wk_col, axis=0, keepdims=True)
            new_carry.append((n_new, m_new))

            mu = jnp.mean(ht, axis=1, keepdims=True)
            var = jnp.mean(jnp.square(ht - mu), axis=1, keepdims=True)
            hn = (ht - mu) * lax.rsqrt(var + LN_EPS) * ng_ref[:, h * E:(h + 1) * E]
            og = om_ref[pl.ds(r0, CH), h * E:(h + 1) * E].astype(F32)
            o_ref[pl.ds(r0, CH), h * E:(h + 1) * E] = (jax.nn.sigmoid(og) * hn).astype(o_ref.dtype)
        return tuple(new_carry)

    init = tuple((jnp.zeros((1, E), F32), jnp.zeros((1, 1), F32)) for _ in range(H))
    lax.fori_loop(0, NC, chunk, init)


def _mlstm(xm, vm, om, gates, conv_w, conv_b, w_mq, w_mk, b_igate, b_fgate, mh_norm_g, B, S):
    T, W = xm.shape
    H = MLSTM_HEADS
    E = W // H
    CH = MLSTM_CHUNK
    NC = S // CH
    gcol = gates.reshape(B, NC, CH, 2 * H)
    grow = gcol.transpose(0, 1, 3, 2)
    gb = jnp.concatenate([b_igate, b_fgate]).astype(F32)
    seq = pl.BlockSpec((S, W), lambda b: (b, 0))
    full2 = lambda shape: pl.BlockSpec(shape, lambda b: (0,) * len(shape))
    return pl.pallas_call(
        _mlstm_kernel,
        grid=(B,),
        in_specs=[seq, seq, seq,
                  pl.BlockSpec((None, NC, CH, 2 * H), lambda b: (b, 0, 0, 0)),
                  pl.BlockSpec((None, NC, 2 * H, CH), lambda b: (b, 0, 0, 0)),
                  full2((1, 2 * H)), full2((2 * H, 1)), full2((MLSTM_CONV, W)), full2((1, W)),
                  full2((H, E, E)), full2((H, E, E)), full2((1, W))],
        out_specs=seq,
        out_shape=jax.ShapeDtypeStruct((T, W), BF16),
        scratch_shapes=[pltpu.VMEM((H, S, E), BF16), pltpu.VMEM((H, S, E), BF16), pltpu.VMEM((H, E, E), F32)],
        compiler_params=_cparams("parallel"),
        name="mlstm",
    )(xm, vm, om, gcol, grow, gb[None, :], gb[:, None], conv_w.astype(F32), conv_b[None, :].astype(F32),
      w_mq.astype(BF16), w_mk.astype(BF16), mh_norm_g[None, :].astype(F32))


def _layer_norm_rows(y, g, b):
    mu = jnp.mean(y, axis=1, keepdims=True)
    d = y - mu
    var = jnp.mean(d * d, axis=1, keepdims=True)
    return d * lax.rsqrt(var + LN_EPS) * g + b


def _out_proj_kernel(x_ref, a_ref, m_ref, wa_ref, wm_ref, g_ref, b_ref, o_ref):
    mix = jnp.dot(a_ref[...], wa_ref[...], preferred_element_type=F32)
    mix = mix + jnp.dot(m_ref[...], wm_ref[...], preferred_element_type=F32)
    o_ref[...] = _layer_norm_rows(DEEPNORM_ALPHA * x_ref[...] + mix, g_ref[...], b_ref[...])


def _out_proj_ln(x2, attn, mlstm, w_out, g, b, tm):
    T, D = x2.shape
    Wa = attn.shape[1]
    Wm = mlstm.shape[1]
    wa = w_out[:Wa].astype(BF16)
    wm = w_out[Wa:].astype(BF16)
    return pl.pallas_call(
        _out_proj_kernel,
        grid=(T // tm,),
        in_specs=[pl.BlockSpec((tm, D), lambda i: (i, 0)),
                  pl.BlockSpec((tm, Wa), lambda i: (i, 0)),
                  pl.BlockSpec((tm, Wm), lambda i: (i, 0)),
                  pl.BlockSpec((Wa, D), lambda i: (0, 0)),
                  pl.BlockSpec((Wm, D), lambda i: (0, 0)),
                  pl.BlockSpec((1, D), lambda i: (0, 0)),
                  pl.BlockSpec((1, D), lambda i: (0, 0))],
        out_specs=pl.BlockSpec((tm, D), lambda i: (i, 0)),
        out_shape=jax.ShapeDtypeStruct((T, D), F32),
        compiler_params=_cparams("parallel"),
        name="out_proj_ln",
    )(x2, attn, mlstm, wa, wm, g[None, :].astype(F32), b[None, :].astype(F32))


def _topk_rows(s, k, payload=None):
    R, N = s.shape
    rid = lax.broadcasted_iota(jnp.int32, (R, N), 0).astype(F32)
    vals, idxs = [], []
    for _ in range(k):
        mx = jnp.max(s, axis=0, keepdims=True)
        ix = jnp.min(jnp.where(s == mx, rid, float(R)), axis=0, keepdims=True)
        sel = rid == ix
        vals.append(mx)
        if payload is None:
            idxs.append(ix)
        else:
            idxs.append(jnp.sum(jnp.where(sel, payload, 0.0), axis=0, keepdims=True))
        s = jnp.where(sel, -jnp.inf, s)
    return jnp.concatenate(vals, axis=0), jnp.concatenate(idxs, axis=0)


def _peer_route_kernel(h_ref, wq_ref, keys_ref, eidx_ref, gate_ref, *, sub):
    tm = h_ref.shape[0]
    K = PEER_TOPK
    hb = h_ref[...].astype(BF16)
    qry = jnp.dot(hb, wq_ref[...], preferred_element_type=F32).astype(BF16)
    pairs = [(a, b) for a in range(K) for b in range(K) if (a + 1) * (b + 1) <= K]
    for t0 in range(0, tm, sub):
        e_rows, g_rows = [], []
        for h in range(PEER_HEADS):
            tops = []
            for p in range(2):
                c0 = (h * 2 + p) * LANES
                qhp = qry[t0:t0 + sub, c0:c0 + LANES]
                sc = lax.dot_general(keys_ref[h, p], qhp, (((1,), (1,)), ((), ())),
                                     preferred_element_type=F32)
                tops.append(_topk_rows(sc, K))
            (v1, i1), (v2, i2) = tops
            cand = jnp.concatenate([v1[a:a + 1] + v2[b:b + 1] for a, b in pairs], axis=0)
            cidx = jnp.concatenate([i1[a:a + 1] * PEER_N_KEYS + i2[b:b + 1] for a, b in pairs], axis=0)
            best, eidx = _topk_rows(cand, K, payload=cidx)
            ex = jnp.exp(best - best[0:1])
            g_rows.append(ex / jnp.sum(ex, axis=0, keepdims=True))
            e_rows.append(eidx.astype(jnp.int32))
        eidx_ref[:, t0:t0 + sub] = jnp.concatenate(e_rows, axis=0)
        gate_ref[:, t0:t0 + sub] = jnp.concatenate(g_rows, axis=0)


def _peer_route(h1, w_query, sub_keys, tm):
    T, D = h1.shape
    NQ = w_query.shape[1]
    R = PEER_HEADS * PEER_TOPK
    return pl.pallas_call(
        functools.partial(_peer_route_kernel, sub=LANES),
        grid=(T // tm,),
        in_specs=[pl.BlockSpec((tm, D), lambda i: (i, 0)),
                  pl.BlockSpec((D, NQ), lambda i: (0, 0)),
                  pl.BlockSpec(sub_keys.shape, lambda i: (0, 0, 0, 0))],
        out_specs=[pl.BlockSpec((R, tm), lambda i: (0, i)), pl.BlockSpec((R, tm), lambda i: (0, i))],
        out_shape=[jax.ShapeDtypeStruct((R, T), jnp.int32), jax.ShapeDtypeStruct((R, T), F32)],
        compiler_params=_cparams("parallel"),
        name="peer_route",
    )(h1, w_query.astype(BF16), sub_keys.astype(BF16))


PEER_TOKENS_PER_STEP = 128
PEER_SLOTS = 16
LANE_TILES = 16


def _gelu_exact(x):
    return 0.5 * x * (1.0 + lax.erf(x * (2.0 ** -0.5)))


def _peer_expert_kernel(idx_ref, h_ref, gate_ref, g_ref, b_ref, table_ref, o_ref, buf, sem, acc, *, nsteps):
    tb, D = h_ref.shape
    R = buf.shape[2]
    half = LANE_TILES // 2
    ahead = PEER_SLOTS - 1
    i = pl.program_id(0)

    def start_row(row, slot, k):
        pltpu.make_async_copy(table_ref.at[idx_ref[0, row, k]], buf.at[slot, :, k, :], sem.at[slot]).start(priority=k % 2)

    def wait(slot):
        pltpu.make_async_copy(buf.at[slot], buf.at[slot], sem.at[slot]).wait()

    @pl.when(i == 0)
    def _():
        for s in range(ahead):
            for k in range(R):
                start_row(s, s, k)

    lane_tok = lax.broadcasted_iota(jnp.int32, (1, tb), 1)
    SUB = 8
    npiece = R // SUB
    per_piece = 3
    mid = (R - 2 * npiece * per_piece) // 2

    def token(j, s):
        islot = (s + ahead) % PEER_SLOTS
        pending = iter(range(R))

        def start_rows(n):
            for _ in range(n):
                start_row(j + ahead, islot, next(pending))

        x = h_ref[pl.ds(j, 1), :]
        xb = [jnp.broadcast_to(x[:, c * LANES:(c + 1) * LANES], (SUB, LANES)) for c in range(half)]
        parts = []
        for p in range(npiece):
            rows = slice(p * SUB, (p + 1) * SUB)
            t = buf[s, 0, rows, :] * xb[0]
            for c in range(1, half):
                t = t + buf[s, c, rows, :] * xb[c]
            parts.append(t)
            start_rows(per_piece)
        a = jnp.sum(jnp.concatenate(parts, axis=0), axis=1, keepdims=True)
        gcol = jnp.sum(jnp.where(lane_tok == j, gate_ref[...], 0.0), axis=1, keepdims=True)
        start_rows(mid)
        coef = jnp.broadcast_to(_gelu_exact(a) * gcol, (R, LANES))
        start_rows(R - 2 * npiece * per_piece - mid)
        outs = [None] * half
        for p in range(npiece):
            rows = slice(p * SUB, (p + 1) * SUB)
            cb = coef[rows, :]
            for c in range(half):
                t = buf[s, half + c, rows, :] * cb
                outs[c] = t if outs[c] is None else outs[c] + t
            start_rows(per_piece)
        acc[pl.ds(j, 1), :] = jnp.concatenate([jnp.sum(o, axis=0, keepdims=True) for o in outs], axis=1)

    def group(jj, carry):
        for s in range(PEER_SLOTS):
            wait(s)
            token(jj * PEER_SLOTS + s, s)
        return carry

    lax.fori_loop(0, tb // PEER_SLOTS, group, 0)

    @pl.when(i == nsteps - 1)
    def _():
        for s in range(ahead):
            wait(s)

    o_ref[...] = _layer_norm_rows(DEEPNORM_ALPHA * h_ref[...] + acc[...], g_ref[...], b_ref[...])


def _peer_experts(h1, eidx_t, gate_t, w_down, w_up, g, b):
    T, D = h1.shape
    R = eidx_t.shape[0]
    tb = PEER_TOKENS_PER_STEP
    nsteps = T // tb
    ahead = PEER_SLOTS - 1
    assert tb % PEER_SLOTS == 0 and 2 * D == LANE_TILES * LANES
    table = jnp.concatenate([w_down, w_up], axis=1).reshape(-1, LANE_TILES, LANES)
    eidx = jnp.pad(eidx_t.T, ((0, tb), (0, 0)))
    win = (jnp.arange(nsteps)[:, None] * tb + jnp.arange(tb + ahead)[None, :])
    eidx_win = eidx[win]
    gate_blk = gate_t.reshape(R, nsteps, tb).transpose(1, 0, 2)
    return pl.pallas_call(
        functools.partial(_peer_expert_kernel, nsteps=nsteps),
        grid=(nsteps,),
        in_specs=[pl.BlockSpec((1, tb + ahead, R), lambda i: (i, 0, 0), memory_space=pltpu.SMEM),
                  pl.BlockSpec((tb, D), lambda i: (i, 0)),
                  pl.BlockSpec((None, R, tb), lambda i: (i, 0, 0)),
                  pl.BlockSpec((1, D), lambda i: (0, 0)),
                  pl.BlockSpec((1, D), lambda i: (0, 0)),
                  pl.BlockSpec(memory_space=pl.ANY)],
        out_specs=pl.BlockSpec((tb, D), lambda i: (i, 0)),
        out_shape=jax.ShapeDtypeStruct((T, D), F32),
        scratch_shapes=[pltpu.VMEM((PEER_SLOTS, LANE_TILES, R, LANES), F32),
                        pltpu.SemaphoreType.DMA((PEER_SLOTS,)),
                        pltpu.VMEM((tb, D), F32)],
        compiler_params=_cparams("arbitrary"),
        name="peer_experts",
    )(eidx_win, h1, gate_blk, g[None, :].astype(F32), b[None, :].astype(F32), table)


def kernel(x, w_in, conv_w, conv_b, w_mq, w_mk, b_igate, b_fgate, mh_norm_g, w_out, ln1_g, ln1_b,
           peer_w_query, peer_sub_keys, peer_w_down, peer_w_up, ln2_g, ln2_b):
    B, S, D = x.shape
    T = B * S
    H = MLSTM_HEADS
    h = x.reshape(T, D)
    for l in range(DEPTH):
        n_gate = 2 * H
        w_main = w_in[l][:, :-n_gate].astype(BF16)
        w_gate = jnp.pad(w_in[l][:, -n_gate:], ((0, 0), (0, LANES - n_gate))).astype(BF16)
        q_a, k_a, v_a, x_m, v_m, o_m, gates = _in_proj(h, w_main, w_gate, tm=min(512, T))
        attn = _dilated_attention(q_a, k_a, v_a, B, S)
        mlstm = _mlstm(x_m, v_m, o_m, gates[:, :n_gate], conv_w[l], conv_b[l], w_mq[l], w_mk[l],
                       b_igate[l], b_fgate[l], mh_norm_g[l], B, S)
        h1 = _out_proj_ln(h, attn, mlstm, w_out[l], ln1_g[l], ln1_b[l], tm=min(512, T))
        eidx_t, gate_t = _peer_route(h1, peer_w_query[l], peer_sub_keys[l], tm=min(512, T))
        h = _peer_experts(h1, eidx_t, gate_t, peer_w_down[l], peer_w_up[l], ln2_g[l], ln2_b[l])
    return h.reshape(B, S, D)
```

```python
import functools
import math

import numpy as np
import jax
import jax.numpy as jnp
from jax import lax
from jax.experimental import pallas as pl
from jax.experimental.pallas import tpu as pltpu

F32 = jnp.float32
BF16 = jnp.bfloat16

ATTN_HEAD_DIM = 64
DILATED_PATTERNS = ((128, 1), (512, 4), (2048, 16))
MLSTM_HEADS = 4
MLSTM_CONV = 4
MLSTM_CHUNK = 128
PEER_HEADS = 8
PEER_N_KEYS = 128
PEER_TOPK = 16
DEPTH = 1
DEEPNORM_ALPHA = (2.0 * DEPTH) ** 0.25
LN_EPS = 1e-5

LANES = 128
NEG_BIG = -1e30
VMEM_LIMIT = 56 * 1024 * 1024


def _cparams(*sem):
    return pltpu.CompilerParams(dimension_semantics=sem, vmem_limit_bytes=VMEM_LIMIT)


def _in_proj_kernel(x_ref, w_ref, wg_ref, *out_refs):
    xb = x_ref[...].astype(BF16)
    g_ref = out_refs[-1]
    for i, o_ref in enumerate(out_refs[:-1]):
        n = o_ref.shape[1]
        o_ref[...] = jnp.dot(xb, w_ref[:, i * n:(i + 1) * n], preferred_element_type=F32).astype(o_ref.dtype)
    g_ref[...] = jnp.dot(xb, wg_ref[...], preferred_element_type=F32)


def _in_proj(x2, w_main, w_gate, tm):
    T, D = x2.shape
    n_main = w_main.shape[1]
    n_out = 6
    wcol = n_main // n_out
    out_shape = [jax.ShapeDtypeStruct((T, wcol), BF16) for _ in range(n_out)]
    out_shape.append(jax.ShapeDtypeStruct((T, LANES), F32))
    out_specs = [pl.BlockSpec((tm, wcol), lambda i: (i, 0)) for _ in range(n_out)]
    out_specs.append(pl.BlockSpec((tm, LANES), lambda i: (i, 0)))
    return pl.pallas_call(
        _in_proj_kernel,
        grid=(T // tm,),
        in_specs=[pl.BlockSpec((tm, D), lambda i: (i, 0)),
                  pl.BlockSpec((D, n_main), lambda i: (0, 0)),
                  pl.BlockSpec((D, LANES), lambda i: (0, 0))],
        out_specs=out_specs,
        out_shape=out_shape,
        compiler_params=_cparams("parallel"),
        name="in_proj",
    )(x2, w_main, w_gate)


def _attn_bias_table(S, blk):
    nrel = S // blk
    rel = np.arange(nrel)[:, None, None] * blk
    delta = rel + np.arange(blk)[None, None, :] - np.arange(blk)[None, :, None]
    cnt = np.zeros(delta.shape, np.float64)
    for window, dil in DILATED_PATTERNS:
        cnt += (delta >= 0) & (delta <= window) & (delta % dil == 0)
    bias = np.where(cnt > 0, np.log(np.maximum(cnt, 1.0)), NEG_BIG)
    return jnp.asarray(bias, F32)


def _attn_kernel(q_ref, k_ref, vt_ref, bias_ref, o_ref, *, blk, scale):
    S = q_ref.shape[0]
    nblk = S // blk
    lo = lax.broadcasted_iota(jnp.int32, (blk, LANES), 1) < ATTN_HEAD_DIM
    row_lo = lax.broadcasted_iota(jnp.int32, (LANES, blk), 0) < ATTN_HEAD_DIM

    def q_block(i, carry):
        r0 = pl.multiple_of(i * blk, blk)
        q = q_ref[pl.ds(r0, blk), :]
        qs = (q.astype(F32) * scale).astype(BF16)
        zero = jnp.zeros_like(qs)
        q2 = jnp.concatenate([jnp.where(lo, qs, zero), jnp.where(lo, zero, qs)], axis=0)

        def k_block(j, st):
            m, l, acc = st
            c0 = pl.multiple_of(j * blk, blk)
            kt = k_ref[pl.ds(c0, blk), :]
            s = lax.dot_general(kt, q2, (((1,), (1,)), ((), ())), preferred_element_type=F32)
            b = bias_ref[i - j]
            s = s + jnp.concatenate([b, b], axis=1)
            m_new = jnp.maximum(m, jnp.max(s, axis=0, keepdims=True))
            alpha = jnp.exp(m - m_new)
            p = jnp.exp(s - m_new)
            l = alpha * l + jnp.sum(p, axis=0, keepdims=True)
            acc = alpha * acc + jnp.dot(vt_ref[j], p.astype(BF16), preferred_element_type=F32)
            return m_new, l, acc

        init = (jnp.full((1, 2 * blk), -jnp.inf, F32), jnp.zeros((1, 2 * blk), F32),
                jnp.zeros((LANES, 2 * blk), F32))
        m, l, acc = lax.fori_loop(0, i + 1, k_block, init)
        out = acc / l
        out_t = jnp.where(row_lo, out[:, :blk], out[:, blk:])
        o_ref[pl.ds(r0, blk), :] = out_t.T.astype(o_ref.dtype)
        return carry

    lax.fori_loop(0, nblk, q_block, 0)


def _dilated_attention(q, k, v, B, S):
    T, W = q.shape
    blk = min(256, S)
    nblk = S // blk
    bias = _attn_bias_table(S, blk)
    npair = W // LANES
    vt = v.reshape(B, nblk, blk, npair, LANES).transpose(0, 3, 1, 4, 2)
    spec = pl.BlockSpec((S, LANES), lambda b, h: (b, h))
    return pl.pallas_call(
        functools.partial(_attn_kernel, blk=blk, scale=ATTN_HEAD_DIM ** -0.5),
        grid=(B, npair),
        in_specs=[spec, spec,
                  pl.BlockSpec((None, None, nblk, LANES, blk), lambda b, h: (b, h, 0, 0, 0)),
                  pl.BlockSpec(bias.shape, lambda b, h: (0, 0, 0))],
        out_specs=spec,
        out_shape=jax.ShapeDtypeStruct((T, W), BF16),
        compiler_params=_cparams("parallel", "parallel"),
        name="dilated_attn",
    )(q, k, vt, bias)


def _log_sigmoid(x):
    return jnp.minimum(x, 0.0) - jnp.log1p(jnp.exp(-jnp.abs(x)))


def _mlstm_kernel(xm_ref, vm_ref, om_ref, gcol_ref, grow_ref, gb_row_ref, gb_col_ref, cw_ref, cb_ref,
                  wq_ref, wk_ref, ng_ref, o_ref, q_s, k_s, c_s):
    S, W = xm_ref.shape
    H = MLSTM_HEADS
    E = W // H
    CH = MLSTM_CHUNK
    NC = S // CH
    hi = lax.Precision.HIGHEST

    x32 = xm_ref[...].astype(F32)
    row = lax.broadcasted_iota(jnp.int32, (S, W), 0)
    xc = x32 * cw_ref[MLSTM_CONV - 1:MLSTM_CONV, :] + cb_ref[...]
    for j in range(1, MLSTM_CONV):
        sh = jnp.where(row >= j, pltpu.roll(x32, j, axis=0), 0.0)
        xc = xc + sh * cw_ref[MLSTM_CONV - 1 - j:MLSTM_CONV - j, :]
    xc = (xc * jax.nn.sigmoid(xc)).astype(BF16)
    for h in range(H):
        xh = xc[:, h * E:(h + 1) * E]
        q_s[h] = (jnp.dot(xh, wq_ref[h], preferred_element_type=F32) * (E ** -0.5)).astype(BF16)
        k_s[h] = jnp.dot(xh, wk_ref[h], preferred_element_type=F32).astype(BF16)

    c_s[...] = jnp.zeros_like(c_s)
    ri = lax.broadcasted_iota(jnp.int32, (CH, CH), 0)
    ci = lax.broadcasted_iota(jnp.int32, (CH, CH), 1)
    tril = ri >= ci
    tril_f = tril.astype(F32)
    triu_f = (ri <= ci).astype(F32)

    def chunk(c, carry):
        r0 = pl.multiple_of(c * CH, CH)
        gc = gcol_ref[c] + gb_row_ref[...]
        gr = grow_ref[c] + gb_col_ref[...]
        bc_col = jnp.dot(tril_f, _log_sigmoid(gc), precision=hi, preferred_element_type=F32)
        bc_row = jnp.dot(_log_sigmoid(gr), triu_f, precision=hi, preferred_element_type=F32)
        new_carry = []
        for h in range(H):
            n_prev, m_prev = carry[h]
            bcol = bc_col[:, H + h:H + h + 1]
            brow = bc_row[H + h:H + h + 1, :]
            icol = gc[:, h:h + 1]
            irow = gr[h:h + 1, :]
            qc = q_s[h, pl.ds(r0, CH), :]
            kc = k_s[h, pl.ds(r0, CH), :]
            vc = vm_ref[pl.ds(r0, CH), h * E:(h + 1) * E]
            C = c_s[h]

            dlog = jnp.where(tril, bcol - brow + irow, -jnp.inf)
            m_inter = bcol + m_prev
            m_t = jnp.maximum(m_inter, jnp.max(dlog, axis=1, keepdims=True))
            w_inter = jnp.exp(m_inter - m_t)
            qk = lax.dot_general(qc, kc, (((1,), (1,)), ((), ())), preferred_element_type=F32)
            P = jnp.exp(dlog - m_t) * qk
            inter = lax.dot_general(qc, C.astype(BF16), (((1,), (1,)), ((), ())), preferred_element_type=F32)
            num = w_inter * inter + jnp.dot(P.astype(BF16), vc, preferred_element_type=F32)
            qn = jnp.sum(qc.astype(F32) * n_prev, axis=1, keepdims=True)
            den = w_inter * qn + jnp.sum(P, axis=1, keepdims=True)
            ht = num / jnp.maximum(jnp.abs(den), jnp.exp(-m_t))

            bL = bcol[CH - 1:CH, :]
            g_row = bL - brow + irow
            g_col = bL - bcol + icol
            m_new = jnp.maximum(bL + m_prev, jnp.max(g_row, axis=1, keepdims=True))
            a = jnp.exp(bL + m_prev - m_new)
            wk_col = jnp.exp(g_col - m_new)
            vw = (vc.astype(F32) * wk_col).astype(BF16)
            c_s[h] = a * C + lax.dot_general(vw, kc, (((0,), (0,)), ((), ())), preferred_element_type=F32)
            n_new = a * n_prev + jnp.sum(kc.astype(F32) * wk_col, axis=0, keepdims=True)
            new_carry.append((n_new, m_new))

            mu = jnp.mean(ht, axis=1, keepdims=True)
            var = jnp.mean(jnp.square(ht - mu), axis=1, keepdims=True)
            hn = (ht - mu) * lax.rsqrt(var + LN_EPS) * ng_ref[:, h * E:(h + 1) * E]
            og = om_ref[pl.ds(r0, CH), h * E:(h + 1) * E].astype(F32)
            o_ref[pl.ds(r0, CH), h * E:(h + 1) * E] = (jax.nn.sigmoid(og) * hn).astype(o_ref.dtype)
        return tuple(new_carry)

    init = tuple((jnp.zeros((1, E), F32), jnp.zeros((1, 1), F32)) for _ in range(H))
    lax.fori_loop(0, NC, chunk, init)


def _mlstm(xm, vm, om, gates, conv_w, conv_b, w_mq, w_mk, b_igate, b_fgate, mh_norm_g, B, S):
    T, W = xm.shape
    H = MLSTM_HEADS
    E = W // H
    CH = MLSTM_CHUNK
    NC = S // CH
    gcol = gates.reshape(B, NC, CH, 2 * H)
    grow = gcol.transpose(0, 1, 3, 2)
    gb = jnp.concatenate([b_igate, b_fgate]).astype(F32)
    seq = pl.BlockSpec((S, W), lambda b: (b, 0))
    full2 = lambda shape: pl.BlockSpec(shape, lambda b: (0,) * len(shape))
    return pl.pallas_call(
        _mlstm_kernel,
        grid=(B,),
        in_specs=[seq, seq, seq,
                  pl.BlockSpec((None, NC, CH, 2 * H), lambda b: (b, 0, 0, 0)),
                  pl.BlockSpec((None, NC, 2 * H, CH), lambda b: (b, 0, 0, 0)),
                  full2((1, 2 * H)), full2((2 * H, 1)), full2((MLSTM_CONV, W)), full2((1, W)),
                  full2((H, E, E)), full2((H, E, E)), full2((1, W))],
        out_specs=seq,
        out_shape=jax.ShapeDtypeStruct((T, W), BF16),
        scratch_shapes=[pltpu.VMEM((H, S, E), BF16), pltpu.VMEM((H, S, E), BF16), pltpu.VMEM((H, E, E), F32)],
        compiler_params=_cparams("parallel"),
        name="mlstm",
    )(xm, vm, om, gcol, grow, gb[None, :], gb[:, None], conv_w.astype(F32), conv_b[None, :].astype(F32),
      w_mq.astype(BF16), w_mk.astype(BF16), mh_norm_g[None, :].astype(F32))


def _layer_norm_rows(y, g, b):
    mu = jnp.mean(y, axis=1, keepdims=True)
    d = y - mu
    var = jnp.mean(d * d, axis=1, keepdims=True)
    return d * lax.rsqrt(var + LN_EPS) * g + b


def _out_proj_kernel(x_ref, a_ref, m_ref, wa_ref, wm_ref, g_ref, b_ref, o_ref):
    mix = jnp.dot(a_ref[...], wa_ref[...], preferred_element_type=F32)
    mix = mix + jnp.dot(m_ref[...], wm_ref[...], preferred_element_type=F32)
    o_ref[...] = _layer_norm_rows(DEEPNORM_ALPHA * x_ref[...] + mix, g_ref[...], b_ref[...])


def _out_proj_ln(x2, attn, mlstm, w_out, g, b, tm):
    T, D = x2.shape
    Wa = attn.shape[1]
    Wm = mlstm.shape[1]
    wa = w_out[:Wa].astype(BF16)
    wm = w_out[Wa:].astype(BF16)
    return pl.pallas_call(
        _out_proj_kernel,
        grid=(T // tm,),
        in_specs=[pl.BlockSpec((tm, D), lambda i: (i, 0)),
                  pl.BlockSpec((tm, Wa), lambda i: (i, 0)),
                  pl.BlockSpec((tm, Wm), lambda i: (i, 0)),
                  pl.BlockSpec((Wa, D), lambda i: (0, 0)),
                  pl.BlockSpec((Wm, D), lambda i: (0, 0)),
                  pl.BlockSpec((1, D), lambda i: (0, 0)),
                  pl.BlockSpec((1, D), lambda i: (0, 0))],
        out_specs=pl.BlockSpec((tm, D), lambda i: (i, 0)),
        out_shape=jax.ShapeDtypeStruct((T, D), F32),
        compiler_params=_cparams("parallel"),
        name="out_proj_ln",
    )(x2, attn, mlstm, wa, wm, g[None, :].astype(F32), b[None, :].astype(F32))


def _topk_rows(s, k, payload=None):
    R, N = s.shape
    rid = lax.broadcasted_iota(jnp.int32, (R, N), 0).astype(F32)
    vals, idxs = [], []
    for _ in range(k):
        mx = jnp.max(s, axis=0, keepdims=True)
        ix = jnp.min(jnp.where(s == mx, rid, float(R)), axis=0, keepdims=True)
        sel = rid == ix
        vals.append(mx)
        if payload is None:
            idxs.append(ix)
        else:
            idxs.append(jnp.sum(jnp.where(sel, payload, 0.0), axis=0, keepdims=True))
        s = jnp.where(sel, -jnp.inf, s)
    return jnp.concatenate(vals, axis=0), jnp.concatenate(idxs, axis=0)


def _peer_route_kernel(h_ref, wq_ref, keys_ref, eidx_ref, gate_ref, *, sub):
    tm = h_ref.shape[0]
    K = PEER_TOPK
    hb = h_ref[...].astype(BF16)
    qry = jnp.dot(hb, wq_ref[...], preferred_element_type=F32).astype(BF16)
    pairs = [(a, b) for a in range(K) for b in range(K) if (a + 1) * (b + 1) <= K]
    for t0 in range(0, tm, sub):
        e_rows, g_rows = [], []
        for h in range(PEER_HEADS):
            tops = []
            for p in range(2):
                c0 = (h * 2 + p) * LANES
                qhp = qry[t0:t0 + sub, c0:c0 + LANES]
                sc = lax.dot_general(keys_ref[h, p], qhp, (((1,), (1,)), ((), ())),
                                     preferred_element_type=F32)
                tops.append(_topk_rows(sc, K))
            (v1, i1), (v2, i2) = tops
            cand = jnp.concatenate([v1[a:a + 1] + v2[b:b + 1] for a, b in pairs], axis=0)
            cidx = jnp.concatenate([i1[a:a + 1] * PEER_N_KEYS + i2[b:b + 1] for a, b in pairs], axis=0)
            best, eidx = _topk_rows(cand, K, payload=cidx)
            ex = jnp.exp(best - best[0:1])
            g_rows.append(ex / jnp.sum(ex, axis=0, keepdims=True))
            e_rows.append(eidx.astype(jnp.int32))
        eidx_ref[:, t0:t0 + sub] = jnp.concatenate(e_rows, axis=0)
        gate_ref[:, t0:t0 + sub] = jnp.concatenate(g_rows, axis=0)


def _peer_route(h1, w_query, sub_keys, tm):
    T, D = h1.shape
    NQ = w_query.shape[1]
    R = PEER_HEADS * PEER_TOPK
    return pl.pallas_call(
        functools.partial(_peer_route_kernel, sub=LANES),
        grid=(T // tm,),
        in_specs=[pl.BlockSpec((tm, D), lambda i: (i, 0)),
                  pl.BlockSpec((D, NQ), lambda i: (0, 0)),
                  pl.BlockSpec(sub_keys.shape, lambda i: (0, 0, 0, 0))],
        out_specs=[pl.BlockSpec((R, tm), lambda i: (0, i)), pl.BlockSpec((R, tm), lambda i: (0, i))],
        out_shape=[jax.ShapeDtypeStruct((R, T), jnp.int32), jax.ShapeDtypeStruct((R, T), F32)],
        compiler_params=_cparams("parallel"),
        name="peer_route",
    )(h1, w_query.astype(BF16), sub_keys.astype(BF16))


PEER_TOKENS_PER_STEP = 128
PEER_SLOTS = 8
LANE_TILES = 16


def _gelu_exact(x):
    return 0.5 * x * (1.0 + lax.erf(x * (2.0 ** -0.5)))


def _peer_expert_kernel(idx_ref, h_ref, gate_ref, g_ref, b_ref, table_ref, o_ref, buf, sem, acc, *, nsteps):
    tb, D = h_ref.shape
    R = buf.shape[2]
    half = LANE_TILES // 2
    ahead = PEER_SLOTS - 1
    i = pl.program_id(0)

    def start_row(row, slot, k):
        pltpu.make_async_copy(table_ref.at[idx_ref[0, row, k]], buf.at[slot, :, k, :], sem.at[slot]).start(priority=1 if k % 4 == 3 else 0)

    def wait(slot):
        pltpu.make_async_copy(buf.at[slot], buf.at[slot], sem.at[slot]).wait()

    @pl.when(i == 0)
    def _():
        for s in range(ahead):
            for k in range(R):
                start_row(s, s, k)

    lane_tok = lax.broadcasted_iota(jnp.int32, (1, tb), 1)
    SUB = 8
    npiece = R // SUB
    per_piece = 3
    mid = (R - 2 * npiece * per_piece) // 2

    def token(j, s):
        islot = (s + ahead) % PEER_SLOTS
        pending = iter(range(R))

        def start_rows(n):
            for _ in range(n):
                start_row(j + ahead, islot, next(pending))

        x = h_ref[pl.ds(j, 1), :]
        xb = [jnp.broadcast_to(x[:, c * LANES:(c + 1) * LANES], (SUB, LANES)) for c in range(half)]
        parts = []
        for p in range(npiece):
            rows = slice(p * SUB, (p + 1) * SUB)
            t = buf[s, 0, rows, :] * xb[0]
            for c in range(1, half):
                t = t + buf[s, c, rows, :] * xb[c]
            parts.append(t)
            start_rows(per_piece)
        a = jnp.sum(jnp.concatenate(parts, axis=0), axis=1, keepdims=True)
        gcol = jnp.sum(jnp.where(lane_tok == j, gate_ref[...], 0.0), axis=1, keepdims=True)
        start_rows(mid)
        coef = jnp.broadcast_to(_gelu_exact(a) * gcol, (R, LANES))
        start_rows(R - 2 * npiece * per_piece - mid)
        outs = [None] * half
        for p in range(npiece):
            rows = slice(p * SUB, (p + 1) * SUB)
            cb = coef[rows, :]
            for c in range(half):
                t = buf[s, half + c, rows, :] * cb
                outs[c] = t if outs[c] is None else outs[c] + t
            start_rows(per_piece)
        acc[pl.ds(j, 1), :] = jnp.concatenate([jnp.sum(o, axis=0, keepdims=True) for o in outs], axis=1)

    def group(jj, carry):
        for s in range(PEER_SLOTS):
            wait(s)
            token(jj * PEER_SLOTS + s, s)
        return carry

    lax.fori_loop(0, tb // PEER_SLOTS, group, 0)

    @pl.when(i == nsteps - 1)
    def _():
        for s in range(ahead):
            wait(s)

    o_ref[...] = _layer_norm_rows(DEEPNORM_ALPHA * h_ref[...] + acc[...], g_ref[...], b_ref[...])


def _peer_experts(h1, eidx_t, gate_t, w_down, w_up, g, b):
    T, D = h1.shape
    R = eidx_t.shape[0]
    tb = PEER_TOKENS_PER_STEP
    nsteps = T // tb
    ahead = PEER_SLOTS - 1
    assert tb % PEER_SLOTS == 0 and 2 * D == LANE_TILES * LANES
    table = jnp.concatenate([w_down, w_up], axis=1).reshape(-1, LANE_TILES, LANES)
    eidx = jnp.pad(eidx_t.T, ((0, tb), (0, 0)))
    win = (jnp.arange(nsteps)[:, None] * tb + jnp.arange(tb + ahead)[None, :])
    eidx_win = eidx[win]
    gate_blk = gate_t.reshape(R, nsteps, tb).transpose(1, 0, 2)
    return pl.pallas_call(
        functools.partial(_peer_expert_kernel, nsteps=nsteps),
        grid=(nsteps,),
        in_specs=[pl.BlockSpec((1, tb + ahead, R), lambda i: (i, 0, 0), memory_space=pltpu.SMEM),
                  pl.BlockSpec((tb, D), lambda i: (i, 0)),
                  pl.BlockSpec((None, R, tb), lambda i: (i, 0, 0)),
                  pl.BlockSpec((1, D), lambda i: (0, 0)),
                  pl.BlockSpec((1, D), lambda i: (0, 0)),
                  pl.BlockSpec(memory_space=pl.ANY)],
        out_specs=pl.BlockSpec((tb, D), lambda i: (i, 0)),
        out_shape=jax.ShapeDtypeStruct((T, D), F32),
        scratch_shapes=[pltpu.VMEM((PEER_SLOTS, LANE_TILES, R, LANES), F32),
                        pltpu.SemaphoreType.DMA((PEER_SLOTS,)),
                        pltpu.VMEM((tb, D), F32)],
        compiler_params=_cparams("arbitrary"),
        name="peer_experts",
    )(eidx_win, h1, gate_blk, g[None, :].astype(F32), b[None, :].astype(F32), table)


def kernel(x, w_in, conv_w, conv_b, w_mq, w_mk, b_igate, b_fgate, mh_norm_g, w_out, ln1_g, ln1_b,
           peer_w_query, peer_sub_keys, peer_w_down, peer_w_up, ln2_g, ln2_b):
    B, S, D = x.shape
    T = B * S
    H = MLSTM_HEADS
    h = x.reshape(T, D)
    for l in range(DEPTH):
        n_gate = 2 * H
        w_main = w_in[l][:, :-n_gate].astype(BF16)
        w_gate = jnp.pad(w_in[l][:, -n_gate:], ((0, 0), (0, LANES - n_gate))).astype(BF16)
        q_a, k_a, v_a, x_m, v_m, o_m, gates = _in_proj(h, w_main, w_gate, tm=min(512, T))
        attn = _dilated_attention(q_a, k_a, v_a, B, S)
        mlstm = _mlstm(x_m, v_m, o_m, gates[:, :n_gate], conv_w[l], conv_b[l], w_mq[l], w_mk[l],
                       b_igate[l], b_fgate[l], mh_norm_g[l], B, S)
        h1 = _out_proj_ln(h, attn, mlstm, w_out[l], ln1_g[l], ln1_b[l], tm=min(512, T))
        eidx_t, gate_t = _peer_route(h1, peer_w_query[l], peer_sub_keys[l], tm=min(512, T))
        h = _peer_experts(h1, eidx_t, gate_t, peer_w_down[l], peer_w_up[l], ln2_g[l], ln2_b[l])
    return h.reshape(B, S, D)
```

```python
import functools
import math

import numpy as np
import jax
import jax.numpy as jnp
from jax import lax
from jax.experimental import pallas as pl
from jax.experimental.pallas import tpu as pltpu

F32 = jnp.float32
BF16 = jnp.bfloat16

ATTN_HEAD_DIM = 64
DILATED_PATTERNS = ((128, 1), (512, 4), (2048, 16))
MLSTM_HEADS = 4
MLSTM_CONV = 4
MLSTM_CHUNK = 128
PEER_HEADS = 8
PEER_N_KEYS = 128
PEER_TOPK = 16
DEPTH = 1
DEEPNORM_ALPHA = (2.0 * DEPTH) ** 0.25
LN_EPS = 1e-5

LANES = 128
NEG_BIG = -1e30
VMEM_LIMIT = 56 * 1024 * 1024


def _cparams(*sem):
    return pltpu.CompilerParams(dimension_semantics=sem, vmem_limit_bytes=VMEM_LIMIT)


def _in_proj_kernel(x_ref, w_ref, wg_ref, *out_refs):
    xb = x_ref[...].astype(BF16)
    g_ref = out_refs[-1]
    for i, o_ref in enumerate(out_refs[:-1]):
        n = o_ref.shape[1]
        o_ref[...] = jnp.dot(xb, w_ref[:, i * n:(i + 1) * n], preferred_element_type=F32).astype(o_ref.dtype)
    g_ref[...] = jnp.dot(xb, wg_ref[...], preferred_element_type=F32)


def _in_proj(x2, w_main, w_gate, tm):
    T, D = x2.shape
    n_main = w_main.shape[1]
    n_out = 6
    wcol = n_main // n_out
    out_shape = [jax.ShapeDtypeStruct((T, wcol), BF16) for _ in range(n_out)]
    out_shape.append(jax.ShapeDtypeStruct((T, LANES), F32))
    out_specs = [pl.BlockSpec((tm, wcol), lambda i: (i, 0)) for _ in range(n_out)]
    out_specs.append(pl.BlockSpec((tm, LANES), lambda i: (i, 0)))
    return pl.pallas_call(
        _in_proj_kernel,
        grid=(T // tm,),
        in_specs=[pl.BlockSpec((tm, D), lambda i: (i, 0)),
                  pl.BlockSpec((D, n_main), lambda i: (0, 0)),
                  pl.BlockSpec((D, LANES), lambda i: (0, 0))],
        out_specs=out_specs,
        out_shape=out_shape,
        compiler_params=_cparams("parallel"),
        name="in_proj",
    )(x2, w_main, w_gate)


def _attn_bias_table(S, blk):
    nrel = S // blk
    rel = np.arange(nrel)[:, None, None] * blk
    delta = rel + np.arange(blk)[None, None, :] - np.arange(blk)[None, :, None]
    cnt = np.zeros(delta.shape, np.float64)
    for window, dil in DILATED_PATTERNS:
        cnt += (delta >= 0) & (delta <= window) & (delta % dil == 0)
    bias = np.where(cnt > 0, np.log(np.maximum(cnt, 1.0)), NEG_BIG)
    return jnp.asarray(bias, F32)


def _attn_kernel(q_ref, k_ref, vt_ref, bias_ref, o_ref, *, blk, scale):
    S = q_ref.shape[0]
    nblk = S // blk
    lo = lax.broadcasted_iota(jnp.int32, (blk, LANES), 1) < ATTN_HEAD_DIM
    row_lo = lax.broadcasted_iota(jnp.int32, (LANES, blk), 0) < ATTN_HEAD_DIM

    def q_block(i, carry):
        r0 = pl.multiple_of(i * blk, blk)
        q = q_ref[pl.ds(r0, blk), :]
        qs = (q.astype(F32) * scale).astype(BF16)
        zero = jnp.zeros_like(qs)
        q2 = jnp.concatenate([jnp.where(lo, qs, zero), jnp.where(lo, zero, qs)], axis=0)

        def k_block(j, st):
            m, l, acc = st
            c0 = pl.multiple_of(j * blk, blk)
            kt = k_ref[pl.ds(c0, blk), :]
            s = lax.dot_general(kt, q2, (((1,), (1,)), ((), ())), preferred_element_type=F32)
            b = bias_ref[i - j]
            s = s + jnp.concatenate([b, b], axis=1)
            m_new = jnp.maximum(m, jnp.max(s, axis=0, keepdims=True))
            alpha = jnp.exp(m - m_new)
            p = jnp.exp(s - m_new)
            l = alpha * l + jnp.sum(p, axis=0, keepdims=True)
            acc = alpha * acc + jnp.dot(vt_ref[j], p.astype(BF16), preferred_element_type=F32)
            return m_new, l, acc

        init = (jnp.full((1, 2 * blk), -jnp.inf, F32), jnp.zeros((1, 2 * blk), F32),
                jnp.zeros((LANES, 2 * blk), F32))
        m, l, acc = lax.fori_loop(0, i + 1, k_block, init)
        out = acc / l
        out_t = jnp.where(row_lo, out[:, :blk], out[:, blk:])
        o_ref[pl.ds(r0, blk), :] = out_t.T.astype(o_ref.dtype)
        return carry

    lax.fori_loop(0, nblk, q_block, 0)


def _dilated_attention(q, k, v, B, S):
    T, W = q.shape
    blk = min(256, S)
    nblk = S // blk
    bias = _attn_bias_table(S, blk)
    npair = W // LANES
    vt = v.reshape(B, nblk, blk, npair, LANES).transpose(0, 3, 1, 4, 2)
    spec = pl.BlockSpec((S, LANES), lambda b, h: (b, h))
    return pl.pallas_call(
        functools.partial(_attn_kernel, blk=blk, scale=ATTN_HEAD_DIM ** -0.5),
        grid=(B, npair),
        in_specs=[spec, spec,
                  pl.BlockSpec((None, None, nblk, LANES, blk), lambda b, h: (b, h, 0, 0, 0)),
                  pl.BlockSpec(bias.shape, lambda b, h: (0, 0, 0))],
        out_specs=spec,
        out_shape=jax.ShapeDtypeStruct((T, W), BF16),
        compiler_params=_cparams("parallel", "parallel"),
        name="dilated_attn",
    )(q, k, vt, bias)


def _log_sigmoid(x):
    return jnp.minimum(x, 0.0) - jnp.log1p(jnp.exp(-jnp.abs(x)))


def _mlstm_kernel(xm_ref, vm_ref, om_ref, gcol_ref, grow_ref, gb_row_ref, gb_col_ref, cw_ref, cb_ref,
                  wq_ref, wk_ref, ng_ref, o_ref, q_s, k_s, c_s):
    S, W = xm_ref.shape
    H = MLSTM_HEADS
    E = W // H
    CH = MLSTM_CHUNK
    NC = S // CH
    hi = lax.Precision.HIGHEST

    x32 = xm_ref[...].astype(F32)
    row = lax.broadcasted_iota(jnp.int32, (S, W), 0)
    xc = x32 * cw_ref[MLSTM_CONV - 1:MLSTM_CONV, :] + cb_ref[...]
    for j in range(1, MLSTM_CONV):
        sh = jnp.where(row >= j, pltpu.roll(x32, j, axis=0), 0.0)
        xc = xc + sh * cw_ref[MLSTM_CONV - 1 - j:MLSTM_CONV - j, :]
    xc = (xc * jax.nn.sigmoid(xc)).astype(BF16)
    for h in range(H):
        xh = xc[:, h * E:(h + 1) * E]
        q_s[h] = (jnp.dot(xh, wq_ref[h], preferred_element_type=F32) * (E ** -0.5)).astype(BF16)
        k_s[h] = jnp.dot(xh, wk_ref[h], preferred_element_type=F32).astype(BF16)

    c_s[...] = jnp.zeros_like(c_s)
    ri = lax.broadcasted_iota(jnp.int32, (CH, CH), 0)
    ci = lax.broadcasted_iota(jnp.int32, (CH, CH), 1)
    tril = ri >= ci
    tril_f = tril.astype(F32)
    triu_f = (ri <= ci).astype(F32)

    def chunk(c, carry):
        r0 = pl.multiple_of(c * CH, CH)
        gc = gcol_ref[c] + gb_row_ref[...]
        gr = grow_ref[c] + gb_col_ref[...]
        bc_col = jnp.dot(tril_f, _log_sigmoid(gc), precision=hi, preferred_element_type=F32)
        bc_row = jnp.dot(_log_sigmoid(gr), triu_f, precision=hi, preferred_element_type=F32)
        new_carry = []
        for h in range(H):
            n_prev, m_prev = carry[h]
            bcol = bc_col[:, H + h:H + h + 1]
            brow = bc_row[H + h:H + h + 1, :]
            icol = gc[:, h:h + 1]
            irow = gr[h:h + 1, :]
            qc = q_s[h, pl.ds(r0, CH), :]
            kc = k_s[h, pl.ds(r0, CH), :]
            vc = vm_ref[pl.ds(r0, CH), h * E:(h + 1) * E]
            C = c_s[h]

            dlog = jnp.where(tril, bcol - brow + irow, -jnp.inf)
            m_inter = bcol + m_prev
            m_t = jnp.maximum(m_inter, jnp.max(dlog, axis=1, keepdims=True))
            w_inter = jnp.exp(m_inter - m_t)
            qk = lax.dot_general(qc, kc, (((1,), (1,)), ((), ())), preferred_element_type=F32)
            P = jnp.exp(dlog - m_t) * qk
            inter = lax.dot_general(qc, C.astype(BF16), (((1,), (1,)), ((), ())), preferred_element_type=F32)
            num = w_inter * inter + jnp.dot(P.astype(BF16), vc, preferred_element_type=F32)
            qn = jnp.sum(qc.astype(F32) * n_prev, axis=1, keepdims=True)
            den = w_inter * qn + jnp.sum(P, axis=1, keepdims=True)
            ht = num / jnp.maximum(jnp.abs(den), jnp.exp(-m_t))

            bL = bcol[CH - 1:CH, :]
            g_row = bL - brow + irow
            g_col = bL - bcol + icol
            m_new = jnp.maximum(bL + m_prev, jnp.max(g_row, axis=1, keepdims=True))
            a = jnp.exp(bL + m_prev - m_new)
            wk_col = jnp.exp(g_col - m_new)
            vw = (vc.astype(F32) * wk_col).astype(BF16)
            c_s[h] = a * C + lax.dot_general(vw, kc, (((0,), (0,)), ((), ())), preferred_element_type=F32)
            n_new = a * n_prev + jnp.sum(kc.astype(F32) * wk_col, axis=0, keepdims=True)
            new_carry.append((n_new, m_new))

            mu = jnp.mean(ht, axis=1, keepdims=True)
            var = jnp.mean(jnp.square(ht - mu), axis=1, keepdims=True)
            hn = (ht - mu) * lax.rsqrt(var + LN_EPS) * ng_ref[:, h * E:(h + 1) * E]
            og = om_ref[pl.ds(r0, CH), h * E:(h + 1) * E].astype(F32)
            o_ref[pl.ds(r0, CH), h * E:(h + 1) * E] = (jax.nn.sigmoid(og) * hn).astype(o_ref.dtype)
        return tuple(new_carry)

    init = tuple((jnp.zeros((1, E), F32), jnp.zeros((1, 1), F32)) for _ in range(H))
    lax.fori_loop(0, NC, chunk, init)


def _mlstm(xm, vm, om, gates, conv_w, conv_b, w_mq, w_mk, b_igate, b_fgate, mh_norm_g, B, S):
    T, W = xm.shape
    H = MLSTM_HEADS
    E = W // H
    CH = MLSTM_CHUNK
    NC = S // CH
    gcol = gates.reshape(B, NC, CH, 2 * H)
    grow = gcol.transpose(0, 1, 3, 2)
    gb = jnp.concatenate([b_igate, b_fgate]).astype(F32)
    seq = pl.BlockSpec((S, W), lambda b: (b, 0))
    full2 = lambda shape: pl.BlockSpec(shape, lambda b: (0,) * len(shape))
    return pl.pallas_call(
        _mlstm_kernel,
        grid=(B,),
        in_specs=[seq, seq, seq,
                  pl.BlockSpec((None, NC, CH, 2 * H), lambda b: (b, 0, 0, 0)),
                  pl.BlockSpec((None, NC, 2 * H, CH), lambda b: (b, 0, 0, 0)),
                  full2((1, 2 * H)), full2((2 * H, 1)), full2((MLSTM_CONV, W)), full2((1, W)),
                  full2((H, E, E)), full2((H, E, E)), full2((1, W))],
        out_specs=seq,
        out_shape=jax.ShapeDtypeStruct((T, W), BF16),
        scratch_shapes=[pltpu.VMEM((H, S, E), BF16), pltpu.VMEM((H, S, E), BF16), pltpu.VMEM((H, E, E), F32)],
        compiler_params=_cparams("parallel"),
        name="mlstm",
    )(xm, vm, om, gcol, grow, gb[None, :], gb[:, None], conv_w.astype(F32), conv_b[None, :].astype(F32),
      w_mq.astype(BF16), w_mk.astype(BF16), mh_norm_g[None, :].astype(F32))


def _layer_norm_rows(y, g, b):
    mu = jnp.mean(y, axis=1, keepdims=True)
    d = y - mu
    var = jnp.mean(d * d, axis=1, keepdims=True)
    return d * lax.rsqrt(var + LN_EPS) * g + b


def _topk_rows(s, k, payload=None):
    R, N = s.shape
    rid = lax.broadcasted_iota(jnp.int32, (R, N), 0).astype(F32)
    vals, idxs = [], []
    for _ in range(k):
        mx = jnp.max(s, axis=0, keepdims=True)
        ix = jnp.min(jnp.where(s == mx, rid, float(R)), axis=0, keepdims=True)
        sel = rid == ix
        vals.append(mx)
        if payload is None:
            idxs.append(ix)
        else:
            idxs.append(jnp.sum(jnp.where(sel, payload, 0.0), axis=0, keepdims=True))
        s = jnp.where(sel, -jnp.inf, s)
    return jnp.concatenate(vals, axis=0), jnp.concatenate(idxs, axis=0)


def _route_tile(hb, wq_ref, keys_ref, eidx_ref, gate_ref, sub):
    tm = hb.shape[0]
    K = PEER_TOPK
    qry =jnp.dot(hb, wq_ref[...], preferred_element_type=F32).astype(BF16)
    pairs = [(a, b) for a in range(K) for b in range(K) if (a + 1) * (b + 1) <= K]
    for t0 in range(0, tm, sub):
        e_rows, g_rows = [], []
        for h in range(PEER_HEADS):
            tops = []
            for p in range(2):
                c0 = (h * 2 + p) * LANES
                qhp = qry[t0:t0 + sub, c0:c0 + LANES]
                sc = lax.dot_general(keys_ref[h, p], qhp, (((1,), (1,)), ((), ())),
                                     preferred_element_type=F32)
                tops.append(_topk_rows(sc, K))
            (v1, i1), (v2, i2) = tops
            cand = jnp.concatenate([v1[a:a + 1] + v2[b:b + 1] for a, b in pairs], axis=0)
            cidx = jnp.concatenate([i1[a:a + 1] * PEER_N_KEYS + i2[b:b + 1] for a, b in pairs], axis=0)
            best, eidx = _topk_rows(cand, K, payload=cidx)
            ex = jnp.exp(best - best[0:1])
            g_rows.append(ex / jnp.sum(ex, axis=0, keepdims=True))
            e_rows.append(eidx.astype(jnp.int32))
        eidx_ref[:, t0:t0 + sub] = jnp.concatenate(e_rows, axis=0)
        gate_ref[:, t0:t0 + sub] = jnp.concatenate(g_rows, axis=0)


def _out_proj_route_kernel(x_ref, a_ref, m_ref, wa_ref, wm_ref, g_ref, b_ref, wq_ref, keys_ref,
                           h_ref, eidx_ref, gate_ref, *, sub):
    mix = jnp.dot(a_ref[...], wa_ref[...], preferred_element_type=F32)
    mix = mix + jnp.dot(m_ref[...], wm_ref[...], preferred_element_type=F32)
    h1 = _layer_norm_rows(DEEPNORM_ALPHA * x_ref[...] + mix, g_ref[...], b_ref[...])
    h_ref[...] = h1
    _route_tile(h1.astype(BF16), wq_ref, keys_ref, eidx_ref, gate_ref, sub)


def _out_proj_ln_route(x2, attn, mlstm, w_out, g, b, w_query, sub_keys, tm):
    T, D = x2.shape
    Wa = attn.shape[1]
    Wm = mlstm.shape[1]
    NQ = w_query.shape[1]
    R = PEER_HEADS * PEER_TOPK
    row = lambda w: pl.BlockSpec((tm, w), lambda i: (i, 0))
    const = lambda shape: pl.BlockSpec(shape, lambda i: (0,) * len(shape))
    return pl.pallas_call(
        functools.partial(_out_proj_route_kernel, sub=LANES),
        grid=(T // tm,),
        in_specs=[row(D), row(Wa), row(Wm), const((Wa, D)), const((Wm, D)), const((1, D)), const((1, D)),
                  const((D, NQ)), const(sub_keys.shape)],
        out_specs=[row(D), pl.BlockSpec((R, tm), lambda i: (0, i)), pl.BlockSpec((R, tm), lambda i: (0, i))],
        out_shape=[jax.ShapeDtypeStruct((T, D), F32), jax.ShapeDtypeStruct((R, T), jnp.int32),
                   jax.ShapeDtypeStruct((R, T), F32)],
        compiler_params=_cparams("parallel"),
        name="out_proj_ln_route",
    )(x2, attn, mlstm, w_out[:Wa].astype(BF16), w_out[Wa:].astype(BF16), g[None, :].astype(F32),
      b[None, :].astype(F32), w_query.astype(BF16), sub_keys.astype(BF16))


PEER_TOKENS_PER_STEP = 128
PEER_SLOTS = 8
LANE_TILES = 16


def _gelu_exact(x):
    return 0.5 * x * (1.0 + lax.erf(x * (2.0 ** -0.5)))


def _peer_expert_kernel(idx_ref, h_ref, gate_ref, g_ref, b_ref, table_ref, o_ref, buf, sem, acc, *, nsteps):
    tb, D = h_ref.shape
    R = buf.shape[2]
    half = LANE_TILES // 2
    ahead = PEER_SLOTS - 1
    i = pl.program_id(0)

    def start_row(row, slot, k):
        pltpu.make_async_copy(table_ref.at[idx_ref[0, row, k]], buf.at[slot, :, k, :], sem.at[slot]).start(priority=k % 2)

    def wait(slot):
        pltpu.make_async_copy(buf.at[slot], buf.at[slot], sem.at[slot]).wait()

    @pl.when(i == 0)
    def _():
        for s in range(ahead):
            for k in range(R):
                start_row(s, s, k)

    lane_tok = lax.broadcasted_iota(jnp.int32, (1, tb), 1)
    SUB = 8
    npiece = R // SUB
    per_piece = 3
    mid = (R - 2 * npiece * per_piece) // 2

    def token(j, s):
        islot = (s + ahead) % PEER_SLOTS
        pending = iter(range(R))

        def start_rows(n):
            for _ in range(n):
                start_row(j + ahead, islot, next(pending))

        x = h_ref[pl.ds(j, 1), :]
        xb = [jnp.broadcast_to(x[:, c * LANES:(c + 1) * LANES], (SUB, LANES)) for c in range(half)]
        parts = []
        for p in range(npiece):
            rows = slice(p * SUB, (p + 1) * SUB)
            t = buf[s, 0, rows, :] * xb[0]
            for c in range(1, half):
                t = t + buf[s, c, rows, :] * xb[c]
            parts.append(t)
            start_rows(per_piece)
        a = jnp.sum(jnp.concatenate(parts, axis=0), axis=1, keepdims=True)
        gcol = jnp.sum(jnp.where(lane_tok == j, gate_ref[...], 0.0), axis=1, keepdims=True)
        start_rows(mid)
        coef = jnp.broadcast_to(_gelu_exact(a) * gcol, (R, LANES))
        start_rows(R - 2 * npiece * per_piece - mid)
        outs = [None] * half
        for p in range(npiece):
            rows = slice(p * SUB, (p + 1) * SUB)
            cb = coef[rows, :]
            for c in range(half):
                t = buf[s, half + c, rows, :] * cb
                outs[c] = t if outs[c] is None else outs[c] + t
            start_rows(per_piece)
        acc[pl.ds(j, 1), :] = jnp.concatenate([jnp.sum(o, axis=0, keepdims=True) for o in outs], axis=1)

    def group(jj, carry):
        for s in range(PEER_SLOTS):
            wait(s)
            token(jj * PEER_SLOTS + s, s)
        return carry

    lax.fori_loop(0, tb // PEER_SLOTS, group, 0)

    @pl.when(i == nsteps - 1)
    def _():
        for s in range(ahead):
            wait(s)

    o_ref[...] = _layer_norm_rows(DEEPNORM_ALPHA * h_ref[...] + acc[...], g_ref[...], b_ref[...])


def _peer_experts(h1, eidx_t, gate_t, w_down, w_up, g, b):
    T, D = h1.shape
    R = eidx_t.shape[0]
    tb = PEER_TOKENS_PER_STEP
    nsteps = T // tb
    ahead = PEER_SLOTS - 1
    assert tb % PEER_SLOTS == 0 and 2 * D == LANE_TILES * LANES
    table = jnp.concatenate([w_down, w_up], axis=1).reshape(-1, LANE_TILES, LANES)
    eidx = jnp.pad(eidx_t.T, ((0, tb), (0, 0)))
    win = (jnp.arange(nsteps)[:, None] * tb + jnp.arange(tb + ahead)[None, :])
    eidx_win = eidx[win]
    gate_blk = gate_t.reshape(R, nsteps, tb).transpose(1, 0, 2)
    return pl.pallas_call(
        functools.partial(_peer_expert_kernel, nsteps=nsteps),
        grid=(nsteps,),
        in_specs=[pl.BlockSpec((1, tb + ahead, R), lambda i: (i, 0, 0), memory_space=pltpu.SMEM),
                  pl.BlockSpec((tb, D), lambda i: (i, 0)),
                  pl.BlockSpec((None, R, tb), lambda i: (i, 0, 0)),
                  pl.BlockSpec((1, D), lambda i: (0, 0)),
                  pl.BlockSpec((1, D), lambda i: (0, 0)),
                  pl.BlockSpec(memory_space=pl.ANY)],
        out_specs=pl.BlockSpec((tb, D), lambda i: (i, 0)),
        out_shape=jax.ShapeDtypeStruct((T, D), F32),
        scratch_shapes=[pltpu.VMEM((PEER_SLOTS, LANE_TILES, R, LANES), F32),
                        pltpu.SemaphoreType.DMA((PEER_SLOTS,)),
                        pltpu.VMEM((tb, D), F32)],
        compiler_params=_cparams("arbitrary"),
        name="peer_experts",
    )(eidx_win, h1, gate_blk, g[None, :].astype(F32), b[None, :].astype(F32), table)


def kernel(x, w_in, conv_w, conv_b, w_mq, w_mk, b_igate, b_fgate, mh_norm_g, w_out, ln1_g, ln1_b,
           peer_w_query, peer_sub_keys, peer_w_down, peer_w_up, ln2_g, ln2_b):
    B, S, D = x.shape
    T = B * S
    H = MLSTM_HEADS
    h = x.reshape(T, D)
    for l in range(DEPTH):
        n_gate = 2 * H
        w_main = w_in[l][:, :-n_gate].astype(BF16)
        w_gate = jnp.pad(w_in[l][:, -n_gate:], ((0, 0), (0, LANES - n_gate))).astype(BF16)
        q_a, k_a, v_a, x_m, v_m, o_m, gates = _in_proj(h, w_main, w_gate, tm=min(512, T))
        attn = _dilated_attention(q_a, k_a, v_a, B, S)
        mlstm = _mlstm(x_m, v_m, o_m, gates[:, :n_gate], conv_w[l], conv_b[l], w_mq[l], w_mk[l],
                       b_igate[l], b_fgate[l], mh_norm_g[l], B, S)
        h1, eidx_t, gate_t = _out_proj_ln_route(h, attn, mlstm, w_out[l], ln1_g[l], ln1_b[l],
                                                peer_w_query[l], peer_sub_keys[l], tm=min(512, T))
        h = _peer_experts(h1, eidx_t, gate_t, peer_w_down[l], peer_w_up[l], ln2_g[l], ln2_b[l])
    return h.reshape(B, S, D)
```
